```python
import jax, jax.numpy as jnp
from jax import lax
import numpy as np

D_MODEL = 2048
BATCH = 8
SEQ = 8192
DEPTH = 4

CHUNK = 64
N_A_LAYERS = DEPTH // 2
N_B_LAYERS = DEPTH - N_A_LAYERS
POOL_WINDOWS = (2, 4, 8, 16)
N_POOL_GROUPS = len(POOL_WINDOWS)
POOL_GROUP = D_MODEL // N_POOL_GROUPS
N_HEADS = 16
HEAD_DIM = D_MODEL // N_HEADS
LEFT_CHUNKS = 8
LEFT = LEFT_CHUNKS * CHUNK
BAND = (LEFT_CHUNKS + 1) * CHUNK
REL_MAX = 128
N_REL = (CHUNK - 1) + REL_MAX + 1
D_FF = 5504
EPS = 1e-6
NEG_INF = -1e30

kernel_name = "yoco_pool_chunked_relbias_macaron"


def rmsnorm(x, g):
    xf = x.astype(jnp.float32)
    y = xf * lax.rsqrt(jnp.mean(xf * xf, axis=-1, keepdims=True) + EPS)
    return (y * g.astype(jnp.float32)).astype(x.dtype)


def swiglu(h, w_gate, w_up, w_down):
    return (jax.nn.silu(h @ w_gate) * (h @ w_up)) @ w_down


def pool_mixer(h, w_pool, scale):
    B, S, D = h.shape
    hf = h.astype(jnp.float32).reshape(B, S, N_POOL_GROUPS, POOL_GROUP)
    cs = jnp.cumsum(hf, axis=1)
    t = jnp.arange(S)
    pooled = []
    for g, w in enumerate(POOL_WINDOWS):
        csg = cs[:, :, g]
        prev = jnp.pad(csg, ((0, 0), (w, 0), (0, 0)))[:, :S]
        cnt = jnp.minimum(t + 1, w).astype(jnp.float32)[None, :, None]
        pooled.append((csg - prev) / cnt)
    pooled = jnp.stack(pooled, axis=2)
    diff = (pooled - hf).astype(h.dtype)
    y = jnp.einsum('bsgc,gcd->bsgd', diff, w_pool).reshape(B, S, D)
    return y * scale


def head_rmsnorm(t, g):
    tf = t.astype(jnp.float32)
    y = tf * lax.rsqrt(jnp.mean(tf * tf, axis=-1, keepdims=True) + EPS)
    return (y * g.astype(jnp.float32)).astype(t.dtype)


def shared_kv(x, kv_norm, w_k, w_v, k_gain):
    B, S, _ = x.shape
    hk = rmsnorm(x, kv_norm)
    k = head_rmsnorm((hk @ w_k).reshape(B, S, N_HEADS, HEAD_DIM), k_gain)
    v = (hk @ w_v).reshape(B, S, N_HEADS, HEAD_DIM)
    pad = ((0, 0), (LEFT, 0), (0, 0), (0, 0))
    return jnp.pad(k, pad), jnp.pad(v, pad)


def chunked_attention(h, w_q, q_gain, rel_table, w_o, k_pad, v_pad):
    B, S, D = h.shape
    nc = S // CHUNK
    q = head_rmsnorm((h @ w_q).reshape(B, S, N_HEADS, HEAD_DIM), q_gain)
    qc = q.reshape(B, nc, CHUNK, N_HEADS, HEAD_DIM).transpose(1, 0, 2, 3, 4)
    r = jnp.arange(CHUNK)[:, None]
    m = jnp.arange(BAND)[None, :]
    rel = r - m + LEFT
    idx = jnp.clip(rel, -(CHUNK - 1), REL_MAX) + (CHUNK - 1)
    bias = rel_table.astype(jnp.float32)[:, idx]
    scale = HEAD_DIM ** -0.5
    key_off = jnp.arange(BAND)

    def one_chunk(args):
        q_blk, c = args
        start = c * CHUNK
        kb = lax.dynamic_slice_in_dim(k_pad, start, BAND, axis=1)
        vb = lax.dynamic_slice_in_dim(v_pad, start, BAND, axis=1)
        s = jnp.einsum('bqhd,bkhd->bhqk', q_blk.astype(jnp.float32), kb.astype(jnp.float32)) * scale
        s = s + bias[None]
        valid = (start + key_off) >= LEFT
        s = jnp.where(valid[None, None, None, :], s, NEG_INF)
        p = jax.nn.softmax(s, axis=-1)
        o = jnp.einsum('bhqk,bkhd->bqhd', p, vb.astype(jnp.float32))
        return o.astype(h.dtype)

    out = lax.map(one_chunk, (qc, jnp.arange(nc)))
    out = out.transpose(1, 0, 2, 3, 4).reshape(B, S, D)
    return out @ w_o


def _fwd_setup_inputs(seed: int = 0) -> dict:
    key = jax.random.key(seed)
    ks = jax.random.split(key, 24)
    f32 = jnp.float32

    def nrm(k, shape, s):
        return jax.random.normal(k, shape, f32) * s

    def gain(k, shape):
        return jnp.ones(shape, f32) + 0.05 * jax.random.normal(k, shape, f32)

    return {
        "x": jax.random.normal(ks[0], (BATCH, SEQ, D_MODEL), f32),
        "ffn1_norm": gain(ks[1], (DEPTH, D_MODEL)),
        "ffn1_w_gate": nrm(ks[2], (DEPTH, D_MODEL, D_FF), D_MODEL ** -0.5),
        "ffn1_w_up": nrm(ks[3], (DEPTH, D_MODEL, D_FF), D_MODEL ** -0.5),
        "ffn1_w_down": nrm(ks[4], (DEPTH, D_FF, D_MODEL), D_FF ** -0.5),
        "mix_norm": gain(ks[5], (DEPTH, D_MODEL)),
        "ffn2_norm": gain(ks[6], (DEPTH, D_MODEL)),
        "ffn2_w_gate": nrm(ks[7], (DEPTH, D_MODEL, D_FF), D_MODEL ** -0.5),
        "ffn2_w_up": nrm(ks[8], (DEPTH, D_MODEL, D_FF), D_MODEL ** -0.5),
        "ffn2_w_down": nrm(ks[9], (DEPTH, D_FF, D_MODEL), D_FF ** -0.5),
        "pool_w": nrm(ks[10], (N_A_LAYERS, N_POOL_GROUPS, POOL_GROUP, POOL_GROUP), POOL_GROUP ** -0.5),
        "pool_scale": gain(ks[11], (N_A_LAYERS, D_MODEL)),
        "kv_norm": gain(ks[12], (D_MODEL,)),
        "w_k": nrm(ks[13], (D_MODEL, D_MODEL), D_MODEL ** -0.5),
        "w_v": nrm(ks[14], (D_MODEL, D_MODEL), D_MODEL ** -0.5),
        "k_gain": gain(ks[15], (HEAD_DIM,)),
        "w_q": nrm(ks[16], (N_B_LAYERS, D_MODEL, D_MODEL), D_MODEL ** -0.5),
        "q_gain": gain(ks[17], (N_B_LAYERS, HEAD_DIM)),
        "rel_bias": nrm(ks[18], (N_B_LAYERS, N_HEADS, N_REL), 0.5),
        "w_o": nrm(ks[19], (N_B_LAYERS, D_MODEL, D_MODEL), D_MODEL ** -0.5),
    }


def _fwd_reference(x, ffn1_norm, ffn1_w_gate, ffn1_w_up, ffn1_w_down, mix_norm, ffn2_norm,
              ffn2_w_gate, ffn2_w_up, ffn2_w_down, pool_w, pool_scale, kv_norm, w_k, w_v,
              k_gain, w_q, q_gain, rel_bias, w_o):
    k_pad, v_pad = None, None
    for l in range(DEPTH):
        x = x + 0.5 * swiglu(rmsnorm(x, ffn1_norm[l]), ffn1_w_gate[l], ffn1_w_up[l], ffn1_w_down[l])
        h = rmsnorm(x, mix_norm[l])
        if l < N_A_LAYERS:
            x = x + pool_mixer(h, pool_w[l], pool_scale[l])
        else:
            b = l - N_A_LAYERS
            x = x + chunked_attention(h, w_q[b], q_gain[b], rel_bias[b], w_o[b], k_pad, v_pad)
        x = x + 0.5 * swiglu(rmsnorm(x, ffn2_norm[l]), ffn2_w_gate[l], ffn2_w_up[l], ffn2_w_down[l])
        if l == N_A_LAYERS - 1:
            k_pad, v_pad = shared_kv(x, kv_norm, w_k, w_v, k_gain)
    return x


import jax as _jax
import jax.numpy as _jnp

TWIN_FORMAT = 'train_step'
FWD_PARAMS = ['x', 'ffn1_norm', 'ffn1_w_gate', 'ffn1_w_up', 'ffn1_w_down', 'mix_norm', 'ffn2_norm', 'ffn2_w_gate', 'ffn2_w_up', 'ffn2_w_down', 'pool_w', 'pool_scale', 'kv_norm', 'w_k', 'w_v', 'k_gain', 'w_q', 'q_gain', 'rel_bias', 'w_o']
TWIN_WEIGHTS = ['ffn1_norm', 'ffn1_w_gate', 'ffn1_w_up', 'ffn1_w_down', 'mix_norm', 'ffn2_norm', 'ffn2_w_gate', 'ffn2_w_up', 'ffn2_w_down', 'pool_w', 'pool_scale', 'kv_norm', 'w_k', 'w_v', 'k_gain', 'w_q', 'q_gain', 'rel_bias', 'w_o']
TWIN_DIFF_INPUT = 'x'
TWIN_INPUTS = ['x', 'ffn1_norm', 'ffn1_w_gate', 'ffn1_w_up', 'ffn1_w_down', 'mix_norm', 'ffn2_norm', 'ffn2_w_gate', 'ffn2_w_up', 'ffn2_w_down', 'pool_w', 'pool_scale', 'kv_norm', 'w_k', 'w_v', 'k_gain', 'w_q', 'q_gain', 'rel_bias', 'w_o', 'loss_target', 'm_ffn1_norm', 'm_ffn1_w_gate', 'm_ffn1_w_up', 'm_ffn1_w_down', 'm_mix_norm', 'm_ffn2_norm', 'm_ffn2_w_gate', 'm_ffn2_w_up', 'm_ffn2_w_down', 'm_pool_w', 'm_pool_scale', 'm_kv_norm', 'm_w_k', 'm_w_v', 'm_k_gain', 'm_w_q', 'm_q_gain', 'm_rel_bias', 'm_w_o', 'v_ffn1_norm', 'v_ffn1_w_gate', 'v_ffn1_w_up', 'v_ffn1_w_down', 'v_mix_norm', 'v_ffn2_norm', 'v_ffn2_w_gate', 'v_ffn2_w_up', 'v_ffn2_w_down', 'v_pool_w', 'v_pool_scale', 'v_kv_norm', 'v_w_k', 'v_w_v', 'v_k_gain', 'v_w_q', 'v_q_gain', 'v_rel_bias', 'v_w_o']
TWIN_OUTPUTS = ['loss', 'grad_x', 'grad_ffn1_norm', 'grad_ffn1_w_gate', 'grad_ffn1_w_up', 'grad_ffn1_w_down', 'grad_mix_norm', 'grad_ffn2_norm', 'grad_ffn2_w_gate', 'grad_ffn2_w_up', 'grad_ffn2_w_down', 'grad_pool_w', 'grad_pool_scale', 'grad_kv_norm', 'grad_w_k', 'grad_w_v', 'grad_k_gain', 'grad_w_q', 'grad_q_gain', 'grad_rel_bias', 'grad_w_o', 'delta_ffn1_norm', 'delta_ffn1_w_gate', 'delta_ffn1_w_up', 'delta_ffn1_w_down', 'delta_mix_norm', 'delta_ffn2_norm', 'delta_ffn2_w_gate', 'delta_ffn2_w_up', 'delta_ffn2_w_down', 'delta_pool_w', 'delta_pool_scale', 'delta_kv_norm', 'delta_w_k', 'delta_w_v', 'delta_k_gain', 'delta_w_q', 'delta_q_gain', 'delta_rel_bias', 'delta_w_o', 'new_m_ffn1_norm', 'new_m_ffn1_w_gate', 'new_m_ffn1_w_up', 'new_m_ffn1_w_down', 'new_m_mix_norm', 'new_m_ffn2_norm', 'new_m_ffn2_w_gate', 'new_m_ffn2_w_up', 'new_m_ffn2_w_down', 'new_m_pool_w', 'new_m_pool_scale', 'new_m_kv_norm', 'new_m_w_k', 'new_m_w_v', 'new_m_k_gain', 'new_m_w_q', 'new_m_q_gain', 'new_m_rel_bias', 'new_m_w_o', 'new_v_ffn1_norm', 'new_v_ffn1_w_gate', 'new_v_ffn1_w_up', 'new_v_ffn1_w_down', 'new_v_mix_norm', 'new_v_ffn2_norm', 'new_v_ffn2_w_gate', 'new_v_ffn2_w_up', 'new_v_ffn2_w_down', 'new_v_pool_w', 'new_v_pool_scale', 'new_v_kv_norm', 'new_v_w_k', 'new_v_w_v', 'new_v_k_gain', 'new_v_w_q', 'new_v_q_gain', 'new_v_rel_bias', 'new_v_w_o']
TWIN_LEAF_KINDS = {'loss': 'loss', 'grad_x': 'grad_x', 'grad_ffn1_norm': 'grad_w', 'grad_ffn1_w_gate': 'grad_w', 'grad_ffn1_w_up': 'grad_w', 'grad_ffn1_w_down': 'grad_w', 'grad_mix_norm': 'grad_w', 'grad_ffn2_norm': 'grad_w', 'grad_ffn2_w_gate': 'grad_w', 'grad_ffn2_w_up': 'grad_w', 'grad_ffn2_w_down': 'grad_w', 'grad_pool_w': 'grad_w', 'grad_pool_scale': 'grad_w', 'grad_kv_norm': 'grad_w', 'grad_w_k': 'grad_w', 'grad_w_v': 'grad_w', 'grad_k_gain': 'grad_w', 'grad_w_q': 'grad_w', 'grad_q_gain': 'grad_w', 'grad_rel_bias': 'grad_w', 'grad_w_o': 'grad_w', 'delta_ffn1_norm': 'delta_w', 'delta_ffn1_w_gate': 'delta_w', 'delta_ffn1_w_up': 'delta_w', 'delta_ffn1_w_down': 'delta_w', 'delta_mix_norm': 'delta_w', 'delta_ffn2_norm': 'delta_w', 'delta_ffn2_w_gate': 'delta_w', 'delta_ffn2_w_up': 'delta_w', 'delta_ffn2_w_down': 'delta_w', 'delta_pool_w': 'delta_w', 'delta_pool_scale': 'delta_w', 'delta_kv_norm': 'delta_w', 'delta_w_k': 'delta_w', 'delta_w_v': 'delta_w', 'delta_k_gain': 'delta_w', 'delta_w_q': 'delta_w', 'delta_q_gain': 'delta_w', 'delta_rel_bias': 'delta_w', 'delta_w_o': 'delta_w', 'new_m_ffn1_norm': 'new_m', 'new_m_ffn1_w_gate': 'new_m', 'new_m_ffn1_w_up': 'new_m', 'new_m_ffn1_w_down': 'new_m', 'new_m_mix_norm': 'new_m', 'new_m_ffn2_norm': 'new_m', 'new_m_ffn2_w_gate': 'new_m', 'new_m_ffn2_w_up': 'new_m', 'new_m_ffn2_w_down': 'new_m', 'new_m_pool_w': 'new_m', 'new_m_pool_scale': 'new_m', 'new_m_kv_norm': 'new_m', 'new_m_w_k': 'new_m', 'new_m_w_v': 'new_m', 'new_m_k_gain': 'new_m', 'new_m_w_q': 'new_m', 'new_m_q_gain': 'new_m', 'new_m_rel_bias': 'new_m', 'new_m_w_o': 'new_m', 'new_v_ffn1_norm': 'new_v', 'new_v_ffn1_w_gate': 'new_v', 'new_v_ffn1_w_up': 'new_v', 'new_v_ffn1_w_down': 'new_v', 'new_v_mix_norm': 'new_v', 'new_v_ffn2_norm': 'new_v', 'new_v_ffn2_w_gate': 'new_v', 'new_v_ffn2_w_up': 'new_v', 'new_v_ffn2_w_down': 'new_v', 'new_v_pool_w': 'new_v', 'new_v_pool_scale': 'new_v', 'new_v_kv_norm': 'new_v', 'new_v_w_k': 'new_v', 'new_v_w_v': 'new_v', 'new_v_k_gain': 'new_v', 'new_v_w_q': 'new_v', 'new_v_q_gain': 'new_v', 'new_v_rel_bias': 'new_v', 'new_v_w_o': 'new_v'}


def _forward(args):
    return _fwd_reference(*[args[k] for k in FWD_PARAMS])


def _output_shape():
    def fwd():
        inp = _fwd_setup_inputs(0)
        return _fwd_reference(*[inp[k] for k in FWD_PARAMS])
    out = _jax.eval_shape(fwd)
    return out.shape, out.dtype

N_MICROBATCH = 1
ADAM_LR = 0.001
ADAM_B1 = 0.9
ADAM_B2 = 0.999
ADAM_EPS = 1e-08
ADAM_WD = 0.01
ADAM_STEP = 10
PER_EXAMPLE_BATCH_AXIS = {'x': 0, 'loss_target': 0}
SHARED_INPUTS = []
_WEIGHT_DTYPES = {'ffn1_norm': _jnp.float32, 'ffn1_w_gate': _jnp.float32, 'ffn1_w_up': _jnp.float32, 'ffn1_w_down': _jnp.float32, 'mix_norm': _jnp.float32, 'ffn2_norm': _jnp.float32, 'ffn2_w_gate': _jnp.float32, 'ffn2_w_up': _jnp.float32, 'ffn2_w_down': _jnp.float32, 'pool_w': _jnp.float32, 'pool_scale': _jnp.float32, 'kv_norm': _jnp.float32, 'w_k': _jnp.float32, 'w_v': _jnp.float32, 'k_gain': _jnp.float32, 'w_q': _jnp.float32, 'q_gain': _jnp.float32, 'rel_bias': _jnp.float32, 'w_o': _jnp.float32}
MOMENT_SCALE = {'ffn1_norm': 6.085707e+00, 'ffn1_w_gate': 7.391806e-02, 'ffn1_w_up': 9.294436e-02, 'ffn1_w_down': 1.526254e-01, 'mix_norm': 1.884483e+01, 'ffn2_norm': 6.162951e+00, 'ffn2_w_gate': 6.187061e-02, 'ffn2_w_up': 8.586783e-02, 'ffn2_w_down': 1.384668e-01, 'pool_w': 2.124987e+00, 'pool_scale': 2.663692e+01, 'kv_norm': 2.689784e-01, 'w_k': 6.593120e-02, 'w_v': 6.389366e-02, 'k_gain': 3.713070e+00, 'w_q': 4.663675e-02, 'q_gain': 1.868750e+00, 'rel_bias': 9.299342e-02, 'w_o': 4.338910e-02}


def _to_microbatches(a, axis):
    t = _jnp.moveaxis(a, axis, 0)
    t = t.reshape((N_MICROBATCH, t.shape[0] // N_MICROBATCH) + t.shape[1:])
    return _jnp.moveaxis(t, 1, axis + 1)


def setup_inputs(seed: int = 0) -> dict:
    inp = _fwd_setup_inputs(seed)
    key = _jax.random.fold_in(_jax.random.key(seed), 7919)
    shape, _ = _output_shape()
    out = dict(inp)
    out["loss_target"] = _jax.random.normal(_jax.random.fold_in(key, 0), shape, _jnp.float32)
    for i, name in enumerate(TWIN_WEIGHTS):
        w = inp[name].astype(_jnp.float32)
        if MOMENT_SCALE is None:
            s = _jnp.sqrt(_jnp.mean(_jnp.square(w)) + 1e-30)
        else:
            s = MOMENT_SCALE[name]
        km, kv = _jax.random.split(_jax.random.fold_in(key, i + 1))
        out[name] = w
        out["m_" + name] = s * _jax.random.normal(km, w.shape, _jnp.float32)
        out["v_" + name] = (s * s) * _jax.random.uniform(kv, w.shape, _jnp.float32, 0.5, 1.5)
    if N_MICROBATCH > 1:
        for name, axis in PER_EXAMPLE_BATCH_AXIS.items():
            out[name] = _to_microbatches(out[name], axis)
    return {'x': out['x'], 'ffn1_norm': out['ffn1_norm'], 'ffn1_w_gate': out['ffn1_w_gate'], 'ffn1_w_up': out['ffn1_w_up'], 'ffn1_w_down': out['ffn1_w_down'], 'mix_norm': out['mix_norm'], 'ffn2_norm': out['ffn2_norm'], 'ffn2_w_gate': out['ffn2_w_gate'], 'ffn2_w_up': out['ffn2_w_up'], 'ffn2_w_down': out['ffn2_w_down'], 'pool_w': out['pool_w'], 'pool_scale': out['pool_scale'], 'kv_norm': out['kv_norm'], 'w_k': out['w_k'], 'w_v': out['w_v'], 'k_gain': out['k_gain'], 'w_q': out['w_q'], 'q_gain': out['q_gain'], 'rel_bias': out['rel_bias'], 'w_o': out['w_o'], 'loss_target': out['loss_target'], 'm_ffn1_norm': out['m_ffn1_norm'], 'm_ffn1_w_gate': out['m_ffn1_w_gate'], 'm_ffn1_w_up': out['m_ffn1_w_up'], 'm_ffn1_w_down': out['m_ffn1_w_down'], 'm_mix_norm': out['m_mix_norm'], 'm_ffn2_norm': out['m_ffn2_norm'], 'm_ffn2_w_gate': out['m_ffn2_w_gate'], 'm_ffn2_w_up': out['m_ffn2_w_up'], 'm_ffn2_w_down': out['m_ffn2_w_down'], 'm_pool_w': out['m_pool_w'], 'm_pool_scale': out['m_pool_scale'], 'm_kv_norm': out['m_kv_norm'], 'm_w_k': out['m_w_k'], 'm_w_v': out['m_w_v'], 'm_k_gain': out['m_k_gain'], 'm_w_q': out['m_w_q'], 'm_q_gain': out['m_q_gain'], 'm_rel_bias': out['m_rel_bias'], 'm_w_o': out['m_w_o'], 'v_ffn1_norm': out['v_ffn1_norm'], 'v_ffn1_w_gate': out['v_ffn1_w_gate'], 'v_ffn1_w_up': out['v_ffn1_w_up'], 'v_ffn1_w_down': out['v_ffn1_w_down'], 'v_mix_norm': out['v_mix_norm'], 'v_ffn2_norm': out['v_ffn2_norm'], 'v_ffn2_w_gate': out['v_ffn2_w_gate'], 'v_ffn2_w_up': out['v_ffn2_w_up'], 'v_ffn2_w_down': out['v_ffn2_w_down'], 'v_pool_w': out['v_pool_w'], 'v_pool_scale': out['v_pool_scale'], 'v_kv_norm': out['v_kv_norm'], 'v_w_k': out['v_w_k'], 'v_w_v': out['v_w_v'], 'v_k_gain': out['v_k_gain'], 'v_w_q': out['v_w_q'], 'v_q_gain': out['v_q_gain'], 'v_rel_bias': out['v_rel_bias'], 'v_w_o': out['v_w_o']}


def _loss(weights, diff, rest, loss_target):
    with _jax.named_scope("forward"):
        args = {**rest, TWIN_DIFF_INPUT: diff, **{k: w.astype(_WEIGHT_DTYPES[k]) for k, w in weights.items()}}
        y = _forward(args)
    with _jax.named_scope("loss_head"):
        err = _jnp.square(y.astype(_jnp.float32) - loss_target)
        return 0.5 * _jnp.sum(_jnp.mean(err, axis=-1)) if err.ndim else 0.5 * err


def _adamw(w, g, m, v):
    m = ADAM_B1 * m + (1.0 - ADAM_B1) * g
    v = ADAM_B2 * v + (1.0 - ADAM_B2) * _jnp.square(g)
    m_hat = m / (1.0 - ADAM_B1 ** ADAM_STEP)
    v_hat = v / (1.0 - ADAM_B2 ** ADAM_STEP)
    delta = -ADAM_LR * (m_hat / (_jnp.sqrt(v_hat) + ADAM_EPS) + ADAM_WD * w)
    return delta, m, v


def reference(x, ffn1_norm, ffn1_w_gate, ffn1_w_up, ffn1_w_down, mix_norm, ffn2_norm, ffn2_w_gate, ffn2_w_up, ffn2_w_down, pool_w, pool_scale, kv_norm, w_k, w_v, k_gain, w_q, q_gain, rel_bias, w_o, loss_target, m_ffn1_norm, m_ffn1_w_gate, m_ffn1_w_up, m_ffn1_w_down, m_mix_norm, m_ffn2_norm, m_ffn2_w_gate, m_ffn2_w_up, m_ffn2_w_down, m_pool_w, m_pool_scale, m_kv_norm, m_w_k, m_w_v, m_k_gain, m_w_q, m_q_gain, m_rel_bias, m_w_o, v_ffn1_norm, v_ffn1_w_gate, v_ffn1_w_up, v_ffn1_w_down, v_mix_norm, v_ffn2_norm, v_ffn2_w_gate, v_ffn2_w_up, v_ffn2_w_down, v_pool_w, v_pool_scale, v_kv_norm, v_w_k, v_w_v, v_k_gain, v_w_q, v_q_gain, v_rel_bias, v_w_o):
    given = dict(x=x, ffn1_norm=ffn1_norm, ffn1_w_gate=ffn1_w_gate, ffn1_w_up=ffn1_w_up, ffn1_w_down=ffn1_w_down, mix_norm=mix_norm, ffn2_norm=ffn2_norm, ffn2_w_gate=ffn2_w_gate, ffn2_w_up=ffn2_w_up, ffn2_w_down=ffn2_w_down, pool_w=pool_w, pool_scale=pool_scale, kv_norm=kv_norm, w_k=w_k, w_v=w_v, k_gain=k_gain, w_q=w_q, q_gain=q_gain, rel_bias=rel_bias, w_o=w_o, loss_target=loss_target, m_ffn1_norm=m_ffn1_norm, m_ffn1_w_gate=m_ffn1_w_gate, m_ffn1_w_up=m_ffn1_w_up, m_ffn1_w_down=m_ffn1_w_down, m_mix_norm=m_mix_norm, m_ffn2_norm=m_ffn2_norm, m_ffn2_w_gate=m_ffn2_w_gate, m_ffn2_w_up=m_ffn2_w_up, m_ffn2_w_down=m_ffn2_w_down, m_pool_w=m_pool_w, m_pool_scale=m_pool_scale, m_kv_norm=m_kv_norm, m_w_k=m_w_k, m_w_v=m_w_v, m_k_gain=m_k_gain, m_w_q=m_w_q, m_q_gain=m_q_gain, m_rel_bias=m_rel_bias, m_w_o=m_w_o, v_ffn1_norm=v_ffn1_norm, v_ffn1_w_gate=v_ffn1_w_gate, v_ffn1_w_up=v_ffn1_w_up, v_ffn1_w_down=v_ffn1_w_down, v_mix_norm=v_mix_norm, v_ffn2_norm=v_ffn2_norm, v_ffn2_w_gate=v_ffn2_w_gate, v_ffn2_w_up=v_ffn2_w_up, v_ffn2_w_down=v_ffn2_w_down, v_pool_w=v_pool_w, v_pool_scale=v_pool_scale, v_kv_norm=v_kv_norm, v_w_k=v_w_k, v_w_v=v_w_v, v_k_gain=v_k_gain, v_w_q=v_w_q, v_q_gain=v_q_gain, v_rel_bias=v_rel_bias, v_w_o=v_w_o)
    weights = {n: given[n] for n in TWIN_WEIGHTS}
    shared = {n: given[n] for n in SHARED_INPUTS}
    per_example = {n: given[n] for n in ['x']}
    grad_fn = _jax.value_and_grad(_loss, argnums=(0, 1))

    def one_microbatch(ex, loss_target):
        ex = dict(ex)
        diff = ex.pop(TWIN_DIFF_INPUT)
        return grad_fn(weights, diff, {**shared, **ex}, loss_target)

    if N_MICROBATCH == 1:
        loss, (grad_w, grad_x) = one_microbatch(per_example, given["loss_target"])
    else:
        def body(carry, xs):
            loss_sum, grad_sum = carry
            l_k, (gw_k, gx_k) = one_microbatch(xs[0], xs[1])
            with _jax.named_scope("update"):
                return (loss_sum + l_k, _jax.tree.map(_jnp.add, grad_sum, gw_k)), gx_k

        init = (_jnp.zeros((), _jnp.float32), _jax.tree.map(_jnp.zeros_like, weights))
        (loss, grad_w), grad_x = _jax.lax.scan(body, init, (per_example, given["loss_target"]))
    with _jax.named_scope("update"):
        delta_w, new_m, new_v = {}, {}, {}
        for n in TWIN_WEIGHTS:
            delta_w[n], new_m[n], new_v[n] = _adamw(weights[n], grad_w[n], given["m_" + n], given["v_" + n])
    return (loss, grad_x, *[grad_w[n] for n in TWIN_WEIGHTS], *[delta_w[n] for n in TWIN_WEIGHTS],
            *[new_m[n] for n in TWIN_WEIGHTS], *[new_v[n] for n in TWIN_WEIGHTS])
```

```python
import functools

import jax
import jax.numpy as jnp
from jax import lax
from jax.experimental import pallas as pl
from jax.experimental.pallas import tpu as pltpu

F32 = jnp.float32
BF16 = jnp.bfloat16
MESH = pl.DeviceIdType.MESH

EPS = 1e-6
NEG_INF = -1e30
CHUNK = 64
LEFT = 512
REL_MAX = 128
N_REL = (CHUNK - 1) + REL_MAX + 1
POOL_WINDOWS = (2, 4, 8, 16)
N_A_LAYERS = 2
HEAD_DIM = 128
HALO = 16

ADAM_LR, ADAM_B1, ADAM_B2, ADAM_EPS, ADAM_WD, ADAM_STEP = 0.001, 0.9, 0.999, 1e-08, 0.01, 10

N_DEV = 8
AXES = ("x", "y", "c")
VMEM_LIMIT_BYTES = 56 * 1024 * 1024
TM = 512
TF = 512
BQ = 256
ROW_CHUNK = 128


def _cparams(*sem):
    return pltpu.CompilerParams(dimension_semantics=sem, vmem_limit_bytes=VMEM_LIMIT_BYTES)


def _nn(a, b):
    return jnp.dot(a, b, preferred_element_type=F32)


def _nt(a, b):
    return lax.dot_general(a, b, (((1,), (1,)), ((), ())), preferred_element_type=F32)


def _tn(a, b):
    return lax.dot_general(a, b, (((0,), (0,)), ((), ())), preferred_element_type=F32)


def _rms_fwd(x, g):
    r = lax.rsqrt(jnp.mean(x * x, axis=-1, keepdims=True) + EPS)
    return x * r * g


def _rms_bwd(x, g, dh):
    r = lax.rsqrt(jnp.mean(x * x, axis=-1, keepdims=True) + EPS)
    xn = x * r
    dxn = dh * g
    dx = r * (dxn - xn * jnp.mean(dxn * xn, axis=-1, keepdims=True))
    return dx, jnp.sum(dh * xn, axis=0, keepdims=True)


def _silu_parts(g):
    sg = jax.nn.sigmoid(g)
    return g * sg, sg * (1.0 + g * (1.0 - sg))


def _row_tile(rows, dtype=F32, cap=512):
    mult = 8 * 4 // jnp.dtype(dtype).itemsize
    for t in range(min(rows, cap), mult - 1, -1):
        if rows % t == 0 and t % mult == 0:
            return t
    return rows


_PEER_FLIPS = ((0, 0, 1), (1, 0, 0), (0, 1, 0), (1, 0, 1), (0, 1, 1), (1, 1, 0), (1, 1, 1))


def _me_and_peers():
    x, y, c = (lax.axis_index(a) for a in AXES)
    peers = []
    for fx, fy, fc in _PEER_FLIPS:
        px, py, pc = (1 - x if fx else x), (1 - y if fy else y), (1 - c if fc else c)
        peers.append(((px, py, pc), 4 * px + 2 * py + pc))
    return 4 * x + 2 * y + c, peers


def gather_rows(arrs, pad_rows, name):
    n = len(arrs)
    n_peer = len(_PEER_FLIPS)
    zeros = [jnp.zeros((pad_rows, a.shape[1]), a.dtype) for a in arrs] if pad_rows else []

    def body(*refs):
        ins, zs = refs[:n], refs[n:n + len(zeros)]
        outs = refs[n + len(zeros):2 * n + len(zeros)]
        send, recv, loc = refs[2 * n + len(zeros):]
        me, peers = _me_and_peers()
        started = []
        for a in range(n):
            r = ins[a].shape[0]

            def rows(idx, a=a, r=r):
                return outs[a].at[pl.ds(pl.multiple_of(idx * r, 8), r)]

            cp = pltpu.make_async_copy(ins[a], rows(me), loc.at[2 * a])
            cp.start()
            started.append(cp)
            if pad_rows:
                cp = pltpu.make_async_copy(zs[a], outs[a].at[pl.ds(N_DEV * r, pad_rows)], loc.at[2 * a + 1])
                cp.start()
                started.append(cp)
        sends, recvs = [], []
        for p, (dev, idx) in enumerate(peers):
            for a in range(n):
                r = ins[a].shape[0]
                k = a * n_peer + p
                mine = outs[a].at[pl.ds(pl.multiple_of(me * r, 8), r)]
                theirs = outs[a].at[pl.ds(pl.multiple_of(idx * r, 8), r)]
                cp = pltpu.make_async_remote_copy(ins[a], mine, send.at[k], recv.at[k], device_id=dev, device_id_type=MESH)
                cp.start()
                sends.append(cp)
                recvs.append(pltpu.make_async_remote_copy(ins[a], theirs, send.at[k], recv.at[k], device_id=dev,
                                                          device_id_type=MESH))
        for cp in started:
            cp.wait()
        for cp in sends:
            cp.wait_send()
        for cp in recvs:
            cp.wait_recv()

    any_spec = pl.BlockSpec(memory_space=pl.ANY)
    return pl.pallas_call(
        body,
        out_shape=[jax.ShapeDtypeStruct((N_DEV * a.shape[0] + pad_rows, a.shape[1]), a.dtype) for a in arrs],
        in_specs=[any_spec] * (n + len(zeros)),
        out_specs=[any_spec] * n,
        scratch_shapes=[pltpu.SemaphoreType.DMA((n * n_peer,)), pltpu.SemaphoreType.DMA((n * n_peer,)),
                        pltpu.SemaphoreType.DMA((2 * n,))],
        name=name,
    )(*arrs, *zeros)


def scatter_rows(arrs, rows, name):
    n = len(arrs)
    n_peer = len(_PEER_FLIPS)

    def body(*refs):
        ins, outs = refs[:n], refs[n:2 * n]
        send, recv, loc = refs[2 * n:]
        me, peers = _me_and_peers()
        started = []
        for a in range(n):
            r = outs[a].shape[1]
            cp = pltpu.make_async_copy(ins[a].at[pl.ds(pl.multiple_of(me * r, 8), r)], outs[a].at[me], loc.at[a])
            cp.start()
            started.append(cp)
        sends, recvs = [], []
        for p, (dev, idx) in enumerate(peers):
            for a in range(n):
                r = outs[a].shape[1]
                k = a * n_peer + p
                src = ins[a].at[pl.ds(pl.multiple_of(idx * r, 8), r)]
                cp = pltpu.make_async_remote_copy(src, outs[a].at[me], send.at[k], recv.at[k], device_id=dev,
                                                  device_id_type=MESH)
                cp.start()
                sends.append(cp)
                recvs.append(pltpu.make_async_remote_copy(src, outs[a].at[idx], send.at[k], recv.at[k], device_id=dev,
                                                          device_id_type=MESH))
        for cp in started:
            cp.wait()
        for cp in sends:
            cp.wait_send()
        for cp in recvs:
            cp.wait_recv()

    any_spec = pl.BlockSpec(memory_space=pl.ANY)
    return pl.pallas_call(
        body,
        out_shape=[jax.ShapeDtypeStruct((N_DEV, rows, a.shape[1]), a.dtype) for a in arrs],
        in_specs=[any_spec] * n,
        out_specs=[any_spec] * n,
        scratch_shapes=[pltpu.SemaphoreType.DMA((n * n_peer,)), pltpu.SemaphoreType.DMA((n * n_peer,)),
                        pltpu.SemaphoreType.DMA((n,))],
        name=name,
    )(*arrs)


def sum_slots(land, name="sum_slots"):
    _, rows, cols = land.shape
    tr = _row_tile(rows, land.dtype)

    def body(l_ref, o_ref):
        acc = l_ref[0].astype(F32)
        for s in range(1, N_DEV):
            acc = acc + l_ref[s].astype(F32)
        o_ref[...] = acc

    return pl.pallas_call(
        body, grid=(rows // tr,),
        in_specs=[pl.BlockSpec((N_DEV, tr, cols), lambda i: (0, i, 0))],
        out_specs=pl.BlockSpec((tr, cols), lambda i: (i, 0)),
        out_shape=jax.ShapeDtypeStruct((rows, cols), F32),
        compiler_params=_cparams("parallel"), name=name,
    )(land)


def ffn_fwd(x, gain, wg_t, wu_t, wd):
    t, d = x.shape
    fp = wd.shape[0]
    n_t, n_f = t // TM, fp // TF

    def body(x_ref, g_ref, wg_ref, wu_ref, wd_ref, xo_ref, hn_ref, gate_ref, up_ref, acc_ref):
        k = pl.program_id(1)

        @pl.when(k == 0)
        def _():
            hn_ref[...] = _rms_fwd(x_ref[...], g_ref[...]).astype(BF16)
            acc_ref[...] = jnp.zeros_like(acc_ref)

        h = hn_ref[...]
        g = _nt(h, wg_ref[...])
        u = _nt(h, wu_ref[...])
        gate_ref[...] = g.astype(BF16)
        up_ref[...] = u.astype(BF16)
        act = (g * jax.nn.sigmoid(g) * u).astype(BF16)
        acc_ref[...] += _nn(act, wd_ref[...])

        @pl.when(k == n_f - 1)
        def _():
            xo_ref[...] = x_ref[...] + 0.5 * acc_ref[...]

    tok = pl.BlockSpec((TM, d), lambda i, k: (i, 0))
    wsp = pl.BlockSpec((TF, d), lambda i, k: (k, 0))
    hid = pl.BlockSpec((TM, TF), lambda i, k: (i, k))
    return pl.pallas_call(
        body, grid=(n_t, n_f),
        in_specs=[tok, pl.BlockSpec((1, d), lambda i, k: (0, 0)), wsp, wsp, wsp],
        out_specs=[tok, tok, hid, hid],
        out_shape=[jax.ShapeDtypeStruct((t, d), F32), jax.ShapeDtypeStruct((t, d), BF16),
                   jax.ShapeDtypeStruct((t, fp), BF16), jax.ShapeDtypeStruct((t, fp), BF16)],
        scratch_shapes=[pltpu.VMEM((TM, d), F32)],
        compiler_params=_cparams("parallel", "arbitrary"), name="ffn_fwd",
    )(x, gain, wg_t, wu_t, wd)


def ffn_bwd_x(dxo, x, gain, gate, up, wg_t, wu_t, wd):
    t, d = x.shape
    fp = wd.shape[0]
    n_t, n_f = t // TM, fp // TF

    def body(dxo_ref, x_ref, g_ref, gate_ref, up_ref, wg_ref, wu_ref, wd_ref,
             dxi_ref, dgate_ref, dup_ref, dy_ref, dgain_ref, acc_ref):
        i, k = pl.program_id(0), pl.program_id(1)

        @pl.when(k == 0)
        def _():
            dy_ref[...] = (0.5 * dxo_ref[...]).astype(BF16)
            acc_ref[...] = jnp.zeros_like(acc_ref)

        @pl.when(jnp.logical_and(i == 0, k == 0))
        def _():
            dgain_ref[...] = jnp.zeros_like(dgain_ref)

        dact = _nt(dy_ref[...], wd_ref[...])
        silu, dsilu = _silu_parts(gate_ref[...].astype(F32))
        dgate = (dact * up_ref[...].astype(F32) * dsilu).astype(BF16)
        dup = (dact * silu).astype(BF16)
        dgate_ref[...] = dgate
        dup_ref[...] = dup
        acc_ref[...] += _nn(dgate, wg_ref[...]) + _nn(dup, wu_ref[...])

        @pl.when(k == n_f - 1)
        def _():
            for c in range(TM // ROW_CHUNK):
                rs = slice(c * ROW_CHUNK, (c + 1) * ROW_CHUNK)
                dx, dg = _rms_bwd(x_ref[rs, :], g_ref[...], acc_ref[rs, :])
                dxi_ref[rs, :] = dxo_ref[rs, :] + dx
                dgain_ref[...] += dg

    tok = pl.BlockSpec((TM, d), lambda i, k: (i, 0))
    tok_once = pl.BlockSpec((TM, d), lambda i, k: (i, 0), pipeline_mode=pl.Buffered(1))
    row = pl.BlockSpec((1, d), lambda i, k: (0, 0))
    wsp = pl.BlockSpec((TF, d), lambda i, k: (k, 0))
    hid = pl.BlockSpec((TM, TF), lambda i, k: (i, k))
    return pl.pallas_call(
        body, grid=(n_t, n_f),
        in_specs=[tok_once, tok_once, row, hid, hid, wsp, wsp, wsp],
        out_specs=[tok, hid, hid, tok, row],
        out_shape=[jax.ShapeDtypeStruct((t, d), F32), jax.ShapeDtypeStruct((t, fp), BF16),
                   jax.ShapeDtypeStruct((t, fp), BF16), jax.ShapeDtypeStruct((t, d), BF16),
                   jax.ShapeDtypeStruct((1, d), F32)],
        scratch_shapes=[pltpu.VMEM((TM, d), F32)],
        compiler_params=_cparams("arbitrary", "arbitrary"), name="ffn_bwd_x",
    )(dxo, x, gain, gate, up, wg_t, wu_t, wd)


def ffn_bwd_w(hn, dy, gate, up, dgate, dup):
    t, d = hn.shape
    fp = gate.shape[1]
    n_t, n_f = t // TM, fp // TF

    def body(hn_ref, dy_ref, gate_ref, up_ref, dgate_ref, dup_ref, dwg_ref, dwu_ref, dwd_ref, ag_ref, au_ref, ad_ref):
        i = pl.program_id(1)

        @pl.when(i == 0)
        def _():
            ag_ref[...] = jnp.zeros_like(ag_ref)
            au_ref[...] = jnp.zeros_like(au_ref)
            ad_ref[...] = jnp.zeros_like(ad_ref)

        h = hn_ref[...]
        ag_ref[...] += _tn(dgate_ref[...], h)
        au_ref[...] += _tn(dup_ref[...], h)
        silu, _ = _silu_parts(gate_ref[...].astype(F32))
        act = (silu * up_ref[...].astype(F32)).astype(BF16)
        ad_ref[...] += _tn(act, dy_ref[...])

        @pl.when(i == n_t - 1)
        def _():
            dwg_ref[...] = ag_ref[...].astype(BF16)
            dwu_ref[...] = au_ref[...].astype(BF16)
            dwd_ref[...] = ad_ref[...].astype(BF16)

    tok = pl.BlockSpec((TM, d), lambda k, i: (i, 0))
    hid = pl.BlockSpec((TM, TF), lambda k, i: (i, k))
    wsp = pl.BlockSpec((TF, d), lambda k, i: (k, 0))
    return pl.pallas_call(
        body, grid=(n_f, n_t),
        in_specs=[tok, tok, hid, hid, hid, hid],
        out_specs=[wsp, wsp, wsp],
        out_shape=[jax.ShapeDtypeStruct((fp, d), BF16)] * 3,
        scratch_shapes=[pltpu.VMEM((TF, d), F32)] * 3,
        compiler_params=_cparams("parallel", "arbitrary"), name="ffn_bwd_w",
    )(hn, dy, gate, up, dgate, dup)


def rms_fwd(x, gain):
    t, d = x.shape

    def body(x_ref, g_ref, o_ref):
        o_ref[...] = _rms_fwd(x_ref[...], g_ref[...]).astype(BF16)

    tok = pl.BlockSpec((TM, d), lambda i: (i, 0))
    return pl.pallas_call(
        body, grid=(t // TM,), in_specs=[tok, pl.BlockSpec((1, d), lambda i: (0, 0))], out_specs=tok,
        out_shape=jax.ShapeDtypeStruct((t, d), BF16), compiler_params=_cparams("parallel"), name="rms_fwd",
    )(x, gain)


def rms_bwd(x, gain, dh, dx_in):
    t, d = x.shape

    def body(x_ref, g_ref, dh_ref, dxi_ref, dxo_ref, dg_ref):
        i = pl.program_id(0)

        @pl.when(i == 0)
        def _():
            dg_ref[...] = jnp.zeros_like(dg_ref)

        dx, dg = _rms_bwd(x_ref[...], g_ref[...], dh_ref[...])
        dxo_ref[...] = dxi_ref[...] + dx
        dg_ref[...] += dg

    tok = pl.BlockSpec((TM, d), lambda i: (i, 0))
    row = pl.BlockSpec((1, d), lambda i: (0, 0))
    return pl.pallas_call(
        body, grid=(t // TM,), in_specs=[tok, row, tok, tok], out_specs=[tok, row],
        out_shape=[jax.ShapeDtypeStruct((t, d), F32), jax.ShapeDtypeStruct((1, d), F32)],
        compiler_params=_cparams("arbitrary"), name="rms_bwd",
    )(x, gain, dh, dx_in)


def proj_nn(a, w, *, head_gain=None, residual=None, out_dtype=F32, name="proj_nn"):
    t, kd = a.shape
    n = w.shape[1]
    tn = min(512, n)

    def body(*refs):
        a_ref, w_ref = refs[:2]
        acc = _nn(a_ref[...], w_ref[...])
        if head_gain is not None:
            g_ref, o_ref, on_ref = refs[2:]
            o_ref[...] = acc
            for j in range(tn // HEAD_DIM):
                sl = slice(j * HEAD_DIM, (j + 1) * HEAD_DIM)
                on_ref[:, sl] = _rms_fwd(acc[:, sl], g_ref[...]).astype(BF16)
        elif residual is not None:
            r_ref, o_ref = refs[2:]
            o_ref[...] = r_ref[...] + acc
        else:
            refs[2][...] = acc.astype(out_dtype)

    a_spec = pl.BlockSpec((TM, kd), lambda i, j: (i, 0))
    w_spec = pl.BlockSpec((kd, tn), lambda i, j: (0, j))
    o_spec = pl.BlockSpec((TM, tn), lambda i, j: (i, j))
    ins, in_specs = [a, w], [a_spec, w_spec]
    if head_gain is not None:
        ins.append(head_gain)
        in_specs.append(pl.BlockSpec((1, HEAD_DIM), lambda i, j: (0, 0)))
        out_specs = [o_spec, o_spec]
        out_shape = [jax.ShapeDtypeStruct((t, n), F32), jax.ShapeDtypeStruct((t, n), BF16)]
    elif residual is not None:
        ins.append(residual)
        in_specs.append(o_spec)
        out_specs, out_shape = o_spec, jax.ShapeDtypeStruct((t, n), F32)
    else:
        out_specs, out_shape = o_spec, jax.ShapeDtypeStruct((t, n), out_dtype)
    return pl.pallas_call(
        body, grid=(t // TM, n // tn), in_specs=in_specs, out_specs=out_specs, out_shape=out_shape,
        compiler_params=_cparams("parallel", "parallel"), name=name,
    )(*ins)


def proj_nt(a, w, *, addend=None, out_dtype=F32, name="proj_nt"):
    t, n = a.shape
    kd = w.shape[0]
    tk = min(512, kd)

    def body(*refs):
        a_ref, w_ref = refs[:2]
        acc = _nt(a_ref[...], w_ref[...])
        if addend is not None:
            acc = acc + refs[2][...]
        refs[-1][...] = acc.astype(out_dtype)

    o_spec = pl.BlockSpec((TM, tk), lambda i, j: (i, j))
    ins = [a, w] + ([addend] if addend is not None else [])
    in_specs = [pl.BlockSpec((TM, n), lambda i, j: (i, 0)), pl.BlockSpec((tk, n), lambda i, j: (j, 0))]
    if addend is not None:
        in_specs.append(o_spec)
    return pl.pallas_call(
        body, grid=(t // TM, kd // tk), in_specs=in_specs, out_specs=o_spec,
        out_shape=jax.ShapeDtypeStruct((t, kd), out_dtype),
        compiler_params=_cparams("parallel", "parallel"), name=name,
    )(*ins)


def grad_tn(a, b, name="grad_tn"):
    t, m = a.shape
    n = b.shape[1]
    tmo, tno = min(1024, m), min(1024, n)
    n_t = t // TM

    def body(a_ref, b_ref, o_ref, acc_ref):
        k = pl.program_id(2)

        @pl.when(k == 0)
        def _():
            acc_ref[...] = jnp.zeros_like(acc_ref)

        acc_ref[...] += _tn(a_ref[...], b_ref[...])

        @pl.when(k == n_t - 1)
        def _():
            o_ref[...] = acc_ref[...].astype(BF16)

    return pl.pallas_call(
        body, grid=(m // tmo, n // tno, n_t),
        in_specs=[pl.BlockSpec((TM, tmo), lambda i, j, k: (k, i)), pl.BlockSpec((TM, tno), lambda i, j, k: (k, j))],
        out_specs=pl.BlockSpec((tmo, tno), lambda i, j, k: (i, j)),
        out_shape=jax.ShapeDtypeStruct((m, n), BF16),
        scratch_shapes=[pltpu.VMEM((tmo, tno), F32)],
        compiler_params=_cparams("parallel", "parallel", "arbitrary"), name=name,
    )(a, b)


def head_rms_bwd(dy, pre, gain):
    t, d = pre.shape

    def body(dy_ref, p_ref, g_ref, o_ref, dg_ref):
        i = pl.program_id(0)

        @pl.when(i == 0)
        def _():
            dg_ref[...] = jnp.zeros_like(dg_ref)

        dg_acc = jnp.zeros((1, HEAD_DIM), F32)
        for j in range(d // HEAD_DIM):
            sl = slice(j * HEAD_DIM, (j + 1) * HEAD_DIM)
            dx, dg = _rms_bwd(p_ref[:, sl], g_ref[...], dy_ref[:, sl])
            o_ref[:, sl] = dx.astype(BF16)
            dg_acc = dg_acc + dg
        dg_ref[...] += dg_acc

    tok = pl.BlockSpec((TM, d), lambda i: (i, 0))
    row = pl.BlockSpec((1, HEAD_DIM), lambda i: (0, 0))
    return pl.pallas_call(
        body, grid=(t // TM,), in_specs=[tok, tok, row], out_specs=[tok, row],
        out_shape=[jax.ShapeDtypeStruct((t, d), BF16), jax.ShapeDtypeStruct((1, HEAD_DIM), F32)],
        compiler_params=_cparams("arbitrary"), name="head_rms_bwd",
    )(dy, pre, gain)


def band_bias(table):
    h = table.shape[0]
    far = jnp.broadcast_to(table[:, N_REL - 1:], (h, LEFT - CHUNK))
    w = jnp.concatenate([far, jnp.flip(table[:, :N_REL - 1], axis=1)], axis=1)
    band = jnp.stack([w[:, CHUNK - 1 - r:CHUNK - 1 - r + LEFT + CHUNK] for r in range(CHUNK)], axis=1)
    out = jnp.full((h, BQ, BQ + LEFT), NEG_INF, F32)
    for a in range(BQ // CHUNK):
        out = out.at[:, a * CHUNK:(a + 1) * CHUNK, a * CHUNK:a * CHUNK + LEFT + CHUNK].set(band)
    return out


def _attn_probs(q_ref, k, b_ref, start):
    w = k.shape[0]
    s = _nt(q_ref[...], k) * (HEAD_DIM ** -0.5) + b_ref[...]
    kpos = start + lax.broadcasted_iota(jnp.int32, (1, w), 1)
    s = jnp.where(kpos >= LEFT, s, NEG_INF)
    e = jnp.exp(s - jnp.max(s, axis=-1, keepdims=True))
    return e / jnp.sum(e, axis=-1, keepdims=True)


def attn_fwd(qn, k_pad, v_pad, bias):
    t, d = qn.shape
    tp = k_pad.shape[0]
    w = BQ + LEFT

    def body(q_ref, k_ref, v_ref, b_ref, o_ref):
        start = pl.multiple_of(pl.program_id(1) * BQ, BQ)
        p = _attn_probs(q_ref, k_ref[pl.ds(start, w), :], b_ref, start)
        o_ref[...] = _nn(p.astype(BF16), v_ref[pl.ds(start, w), :]).astype(BF16)

    q_spec = pl.BlockSpec((BQ, HEAD_DIM), lambda h, i: (i, h))
    kv_spec = pl.BlockSpec((tp, HEAD_DIM), lambda h, i: (0, h))
    return pl.pallas_call(
        body, grid=(d // HEAD_DIM, t // BQ),
        in_specs=[q_spec, kv_spec, kv_spec, pl.BlockSpec((None, BQ, w), lambda h, i: (h, 0, 0))],
        out_specs=q_spec, out_shape=jax.ShapeDtypeStruct((t, d), BF16),
        compiler_params=_cparams("parallel", "parallel"), name="attn_fwd",
    )(qn, k_pad, v_pad, bias)


def attn_bwd(qn, k_pad, v_pad, bias, do):
    t, d = qn.shape
    tp = k_pad.shape[0]
    w = BQ + LEFT
    scale = HEAD_DIM ** -0.5

    def body(q_ref, k_ref, v_ref, b_ref, do_ref, dq_ref, dk_ref, dv_ref, db_ref):
        i = pl.program_id(1)
        start = pl.multiple_of(i * BQ, BQ)

        @pl.when(i == 0)
        def _():
            dk_ref[...] = jnp.zeros_like(dk_ref)
            dv_ref[...] = jnp.zeros_like(dv_ref)
            db_ref[...] = jnp.zeros_like(db_ref)

        k = k_ref[pl.ds(start, w), :]
        v = v_ref[pl.ds(start, w), :]
        p = _attn_probs(q_ref, k, b_ref, start)
        dp = _nt(do_ref[...], v)
        ds = p * (dp - jnp.sum(p * dp, axis=-1, keepdims=True))
        db_ref[...] += ds
        dsb = ds.astype(BF16)
        dq_ref[...] = _nn(dsb, k) * scale
        dk_ref[pl.ds(start, w), :] += _tn(dsb, q_ref[...]) * scale
        dv_ref[pl.ds(start, w), :] += _tn(p.astype(BF16), do_ref[...])

    q_spec = pl.BlockSpec((BQ, HEAD_DIM), lambda h, i: (i, h))
    kv_spec = pl.BlockSpec((tp, HEAD_DIM), lambda h, i: (0, h))
    b_spec = pl.BlockSpec((None, BQ, w), lambda h, i: (h, 0, 0))
    return pl.pallas_call(
        body, grid=(d // HEAD_DIM, t // BQ),
        in_specs=[q_spec, kv_spec, kv_spec, b_spec, q_spec],
        out_specs=[q_spec, kv_spec, kv_spec, b_spec],
        out_shape=[jax.ShapeDtypeStruct((t, d), F32), jax.ShapeDtypeStruct((tp, d), F32),
                   jax.ShapeDtypeStruct((tp, d), F32), jax.ShapeDtypeStruct(bias.shape, F32)],
        compiler_params=_cparams("parallel", "arbitrary"), name="attn_bwd",
    )(qn, k_pad, v_pad, bias, do)


def _pool_diff(ext_ref, h, t0, g, cg):
    tm = h.shape[0]
    cs = slice(g * cg, (g + 1) * cg)
    win = POOL_WINDOWS[g]
    s = ext_ref[HALO:HALO + tm, cs]
    for j in range(1, win):
        s = s + ext_ref[HALO - j:HALO - j + tm, cs]
    cnt = jnp.minimum(t0 + lax.broadcasted_iota(jnp.int32, (tm, 1), 0) + 1, win).astype(F32)
    return (s / cnt - h[:, cs]).astype(BF16), cnt


def pool_fwd(x, gain, w, scale):
    t, d = x.shape
    n_g = len(POOL_WINDOWS)
    cg = d // n_g

    def body(x_ref, g_ref, w_ref, s_ref, o_ref, ext_ref):
        i = pl.program_id(0)
        h = _rms_fwd(x_ref[...], g_ref[...])

        @pl.when(i == 0)
        def _():
            ext_ref[0:HALO, :] = jnp.zeros((HALO, d), F32)

        @pl.when(i > 0)
        def _():
            ext_ref[0:HALO, :] = ext_ref[TM:TM + HALO, :]

        ext_ref[HALO:HALO + TM, :] = h
        for g in range(n_g):
            cs = slice(g * cg, (g + 1) * cg)
            diff, _ = _pool_diff(ext_ref, h, i * TM, g, cg)
            o_ref[:, cs] = x_ref[:, cs] + _nn(diff, w_ref[g]) * s_ref[:, cs]

    tok = pl.BlockSpec((TM, d), lambda i: (i, 0))
    row = pl.BlockSpec((1, d), lambda i: (0, 0))
    return pl.pallas_call(
        body, grid=(t // TM,),
        in_specs=[tok, row, pl.BlockSpec((n_g, cg, cg), lambda i: (0, 0, 0)), row],
        out_specs=tok, out_shape=jax.ShapeDtypeStruct((t, d), F32),
        scratch_shapes=[pltpu.VMEM((HALO + TM, d), F32)],
        compiler_params=_cparams("arbitrary"), name="pool_fwd",
    )(x, gain, w, scale)


def pool_bwd(x, dxo, gain, w, scale):
    t, d = x.shape
    n_g = len(POOL_WINDOWS)
    cg = d // n_g
    n_t = t // TM

    def body(x_ref, xh_ref, dxo_ref, g_ref, w_ref, s_ref, dxi_ref, dw_ref, ds_ref, dg_ref, exth_ref, extq_ref, dh_ref):
        i = pl.program_id(0)
        ti = n_t - 1 - i

        @pl.when(i == 0)
        def _():
            dw_ref[...] = jnp.zeros_like(dw_ref)
            ds_ref[...] = jnp.zeros_like(ds_ref)
            dg_ref[...] = jnp.zeros_like(dg_ref)
            extq_ref[TM:TM + HALO, :] = jnp.zeros((HALO, d), F32)

        @pl.when(i > 0)
        def _():
            extq_ref[TM:TM + HALO, :] = extq_ref[0:HALO, :]

        xv = x_ref[...]
        h = _rms_fwd(xv, g_ref[...])
        exth_ref[0:HALO, :] = jnp.where(ti > 0, _rms_fwd(xh_ref[...], g_ref[...]), 0.0)
        exth_ref[HALO:HALO + TM, :] = h
        for g in range(n_g):
            cs = slice(g * cg, (g + 1) * cg)
            diff, cnt = _pool_diff(exth_ref, h, ti * TM, g, cg)
            dxo_g = dxo_ref[:, cs]
            ds_ref[:, cs] += jnp.sum(dxo_g * _nn(diff, w_ref[g]), axis=0, keepdims=True)
            dy = (dxo_g * s_ref[:, cs]).astype(BF16)
            dw_ref[g] += _tn(diff, dy)
            ddiff = _nt(dy, w_ref[g])
            extq_ref[0:TM, cs] = ddiff / cnt
            acc = -ddiff
            for j in range(POOL_WINDOWS[g]):
                acc = acc + extq_ref[j:j + TM, cs]
            dh_ref[:, cs] = acc
        dx, dg = _rms_bwd(xv, g_ref[...], dh_ref[...])
        dxi_ref[...] = dxo_ref[...] + dx
        dg_ref[...] += dg

    tok = pl.BlockSpec((TM, d), lambda i: (n_t - 1 - i, 0))
    halo = pl.BlockSpec((HALO, d), lambda i: (jnp.maximum((n_t - 1 - i) * (TM // HALO) - 1, 0), 0))
    row = pl.BlockSpec((1, d), lambda i: (0, 0))
    wsp = pl.BlockSpec((n_g, cg, cg), lambda i: (0, 0, 0))
    return pl.pallas_call(
        body, grid=(n_t,),
        in_specs=[tok, halo, tok, row, wsp, row],
        out_specs=[tok, wsp, row, row],
        out_shape=[jax.ShapeDtypeStruct((t, d), F32), jax.ShapeDtypeStruct((n_g, cg, cg), F32),
                   jax.ShapeDtypeStruct((1, d), F32), jax.ShapeDtypeStruct((1, d), F32)],
        scratch_shapes=[pltpu.VMEM((HALO + TM, d), F32), pltpu.VMEM((TM + HALO, d), F32), pltpu.VMEM((TM, d), F32)],
        compiler_params=_cparams("arbitrary"), name="pool_bwd",
    )(x, x, dxo, gain, w, scale)


def loss_head(y, target):
    t, d = y.shape

    def body(y_ref, t_ref, sq_ref, dy_ref):
        i = pl.program_id(0)

        @pl.when(i == 0)
        def _():
            sq_ref[...] = jnp.zeros_like(sq_ref)

        err = y_ref[...] - t_ref[...]
        sq_ref[...] += jnp.sum(err * err, axis=0, keepdims=True)
        dy_ref[...] = err * (1.0 / d)

    tok = pl.BlockSpec((TM, d), lambda i: (i, 0))
    row = pl.BlockSpec((1, d), lambda i: (0, 0))
    return pl.pallas_call(
        body, grid=(t // TM,), in_specs=[tok, tok], out_specs=[row, tok],
        out_shape=[jax.ShapeDtypeStruct((1, d), F32), jax.ShapeDtypeStruct((t, d), F32)],
        compiler_params=_cparams("arbitrary"), name="loss_head",
    )(y, target)


def adamw(w, m, v, g, name="adamw"):
    rows, cols = w.shape
    tr = _row_tile(rows)

    def body(w_ref, m_ref, v_ref, g_ref, d_ref, mo_ref, vo_ref):
        gv = g_ref[...]
        mn = ADAM_B1 * m_ref[...] + (1.0 - ADAM_B1) * gv
        vn = ADAM_B2 * v_ref[...] + (1.0 - ADAM_B2) * jnp.square(gv)
        m_hat = mn / (1.0 - ADAM_B1 ** ADAM_STEP)
        v_hat = vn / (1.0 - ADAM_B2 ** ADAM_STEP)
        d_ref[...] = -ADAM_LR * (m_hat / (jnp.sqrt(v_hat) + ADAM_EPS) + ADAM_WD * w_ref[...])
        mo_ref[...] = mn
        vo_ref[...] = vn

    blk = pl.BlockSpec((tr, cols), lambda i: (i, 0))
    return pl.pallas_call(
        body, grid=(rows // tr,), in_specs=[blk] * 4, out_specs=[blk] * 3,
        out_shape=[jax.ShapeDtypeStruct((rows, cols), F32)] * 3,
        compiler_params=_cparams("parallel"), name=name,
    )(w, m, v, g)


def _adamw_nd(w, m, v, g):
    shape = w.shape
    cols = shape[-1]
    d, mn, vn = adamw(w.reshape(-1, cols), m.reshape(-1, cols), v.reshape(-1, cols), g.reshape(-1, cols))
    return d.reshape(shape), mn.reshape(shape), vn.reshape(shape)


def _pack_small(d, parts):
    flat = jnp.concatenate([p.reshape(-1).astype(F32) for p in parts])
    rows = -(-flat.shape[0] // d)
    rows = -(-rows // 8) * 8
    return jnp.pad(flat, (0, rows * d - flat.shape[0])).reshape(rows, d)


def _unpack_small(packed, shapes):
    flat = packed.reshape(-1)
    out, off = [], 0
    for s in shapes:
        n = 1
        for k in s:
            n *= k
        out.append(flat[off:off + n].reshape(s))
        off += n
    return out


def kernel(x, ffn1_norm, ffn1_w_gate, ffn1_w_up, ffn1_w_down, mix_norm, ffn2_norm, ffn2_w_gate, ffn2_w_up, ffn2_w_down, pool_w, pool_scale, kv_norm, w_k, w_v, k_gain, w_q, q_gain, rel_bias, w_o, loss_target, m_ffn1_norm, m_ffn1_w_gate, m_ffn1_w_up, m_ffn1_w_down, m_mix_norm, m_ffn2_norm, m_ffn2_w_gate, m_ffn2_w_up, m_ffn2_w_down, m_pool_w, m_pool_scale, m_kv_norm, m_w_k, m_w_v, m_k_gain, m_w_q, m_q_gain, m_rel_bias, m_w_o, v_ffn1_norm, v_ffn1_w_gate, v_ffn1_w_up, v_ffn1_w_down, v_mix_norm, v_ffn2_norm, v_ffn2_w_gate, v_ffn2_w_up, v_ffn2_w_down, v_pool_w, v_pool_scale, v_kv_norm, v_w_k, v_w_v, v_k_gain, v_w_q, v_q_gain, v_rel_bias, v_w_o):
    x0, target = x[0], loss_target[0]
    t, d = x0.shape
    depth = ffn1_norm.shape[0]
    n_b = depth - N_A_LAYERS
    fs = ffn1_w_gate.shape[-1]
    f_pad = -(-(N_DEV * fs) // TF) * TF - N_DEV * fs
    n_g = len(POOL_WINDOWS)
    cg = d // n_g
    ds_ = d // N_DEV
    me = 4 * lax.axis_index("x") + 2 * lax.axis_index("y") + lax.axis_index("c")

    ffn_in = ((ffn1_norm, ffn1_w_gate, ffn1_w_up, ffn1_w_down), (ffn2_norm, ffn2_w_gate, ffn2_w_up, ffn2_w_down))
    ffn_w = {}
    for s, (_, wg, wu, wd) in enumerate(ffn_in):
        for l in range(depth):
            ffn_w[s, l] = gather_rows([wg[l].T.astype(BF16), wu[l].T.astype(BF16), wd[l].astype(BF16)], f_pad, "gather_ffn")
    scale_rows = jnp.pad(pool_scale, ((0, 8 - pool_scale.shape[0]), (0, 0)))
    mix_loc = [w_k.astype(BF16), w_v.astype(BF16)]
    mix_loc += [w_q[b].astype(BF16) for b in range(n_b)] + [w_o[b].astype(BF16) for b in range(n_b)]
    mix_loc += [pool_w[l].reshape(n_g * cg // N_DEV, cg).astype(BF16) for l in range(N_A_LAYERS)]
    mix_all = gather_rows(mix_loc + [scale_rows], 0, "gather_mix")
    wk_f, wv_f = mix_all[0], mix_all[1]
    wq_f, wo_f = mix_all[2:2 + n_b], mix_all[2 + n_b:2 + 2 * n_b]
    pool_f = [p.reshape(N_DEV, n_g, cg // N_DEV, cg).transpose(1, 0, 2, 3).reshape(n_g, cg, cg)
              for p in mix_all[2 + 2 * n_b:2 + 2 * n_b + N_A_LAYERS]]
    scale_f = mix_all[-1].reshape(N_DEV, 8, ds_)[:, :N_A_LAYERS].transpose(1, 0, 2).reshape(N_A_LAYERS, d)
    bias = [band_bias(rel_bias[b]) for b in range(n_b)]

    saved = []
    xc = x0
    k_pad = v_pad = kv_saved = None
    for l in range(depth):
        rec = {"x_a": xc}
        xb, rec["hn1"], rec["g1"], rec["u1"] = ffn_fwd(xc, ffn1_norm[l:l + 1], *ffn_w[0, l])
        rec["x_b"] = xb
        if l < N_A_LAYERS:
            xm = pool_fwd(xb, mix_norm[l:l + 1], pool_f[l], scale_f[l:l + 1])
        else:
            b = l - N_A_LAYERS
            hm = rms_fwd(xb, mix_norm[l:l + 1])
            q_pre, qn = proj_nn(hm, wq_f[b], head_gain=q_gain[b:b + 1], name="proj_q")
            o = attn_fwd(qn, k_pad, v_pad, bias[b])
            xm = proj_nn(o, wo_f[b], residual=xb, name="proj_o")
            rec.update(hm=hm, q_pre=q_pre, qn=qn, o=o)
        rec["x_c"] = xm
        xc, rec["hn2"], rec["g2"], rec["u2"] = ffn_fwd(xm, ffn2_norm[l:l + 1], *ffn_w[1, l])
        if l == N_A_LAYERS - 1:
            hk = rms_fwd(xc, kv_norm[None])
            k_pre, kn = proj_nn(hk, wk_f, head_gain=k_gain[None], name="proj_q")
            vv = proj_nn(hk, wv_f, out_dtype=BF16, name="proj_v")
            k_pad = jnp.pad(kn, ((LEFT, 0), (0, 0)))
            v_pad = jnp.pad(vv, ((LEFT, 0), (0, 0)))
            kv_saved = dict(x=xc, hk=hk, k_pre=k_pre)
        saved.append(rec)
    sq, dx = loss_head(xc, target)
    loss = lax.psum(0.5 * jnp.sum(sq) / d, AXES)

    g_ffn = {}
    g_norm = {0: [None] * depth, 1: [None] * depth, 2: [None] * depth}
    g_wq, g_wo, g_pool, g_qgain, g_bias, g_pscale = [None] * n_b, [None] * n_b, [None] * N_A_LAYERS, [None] * n_b, [None] * n_b, [None] * N_A_LAYERS
    dk_sum = dv_sum = None

    def ffn_backward(dx, l, s, x_in, hn, gate, up):
        dx, dgate, dup, dy, dgain = ffn_bwd_x(dx, x_in, ffn_in[s][0][l:l + 1], gate, up, *ffn_w[s, l])
        land = scatter_rows(ffn_bwd_w(hn, dy, gate, up, dgate, dup), fs, "scatter_ffn")
        g_ffn[s, l] = [sum_slots(z, "sum_ffn") for z in land]
        g_norm[2 * s][l] = dgain
        return dx

    for l in reversed(range(depth)):
        rec = saved[l]
        dx = ffn_backward(dx, l, 1, rec["x_c"], rec["hn2"], rec["g2"], rec["u2"])
        if l < N_A_LAYERS:
            dx, g_pool[l], g_pscale[l], g_norm[1][l] = pool_bwd(rec["x_b"], dx, mix_norm[l:l + 1], pool_f[l], scale_f[l:l + 1])
        else:
            b = l - N_A_LAYERS
            dxb = dx.astype(BF16)
            do = proj_nt(dxb, wo_f[b], out_dtype=BF16, name="proj_do")
            g_wo[b] = grad_tn(rec["o"], dxb)
            dq, dk, dv, dbias = attn_bwd(rec["qn"], k_pad, v_pad, bias[b], do)
            dk_sum = dk if dk_sum is None else dk_sum + dk
            dv_sum = dv if dv_sum is None else dv_sum + dv
            g_bias[b] = jax.vjp(band_bias, rel_bias[b])[1](dbias)[0]
            dq_pre, g_qgain[b] = head_rms_bwd(dq, rec["q_pre"], q_gain[b:b + 1])
            g_wq[b] = grad_tn(rec["hm"], dq_pre)
            dhm = proj_nt(dq_pre, wq_f[b], name="proj_dh")
            dx, g_norm[1][l] = rms_bwd(rec["x_b"], mix_norm[l:l + 1], dhm, dx)
        dx = ffn_backward(dx, l, 0, rec["x_a"], rec["hn1"], rec["g1"], rec["u1"])
        if l == N_A_LAYERS:
            dk_pre, g_kgain = head_rms_bwd(dk_sum[LEFT:], kv_saved["k_pre"], k_gain[None])
            dvb = dv_sum[LEFT:].astype(BF16)
            g_wk = grad_tn(kv_saved["hk"], dk_pre)
            g_wv = grad_tn(kv_saved["hk"], dvb)
            dhk = proj_nt(dvb, wv_f, addend=proj_nt(dk_pre, wk_f, name="proj_dh"), name="proj_dh2")
            dx, g_kvnorm = rms_bwd(kv_saved["x"], kv_norm[None], dhk, dx)
    grad_x = dx[None]

    mix_g = [g_wk, g_wv] + g_wq + g_wo
    mix_g += [g.reshape(n_g, N_DEV, cg // N_DEV, cg).transpose(1, 0, 2, 3).reshape(n_g * cg, cg).astype(BF16) for g in g_pool]
    mix_land = [sum_slots(z, "sum_mix") for z in scatter_rows(mix_g, ds_, "scatter_mix")]
    grad_w_k, grad_w_v = mix_land[0], mix_land[1]
    grad_w_q = jnp.stack(mix_land[2:2 + n_b])
    grad_w_o = jnp.stack(mix_land[2 + n_b:2 + 2 * n_b])
    grad_pool_w = jnp.stack([g.reshape(n_g, cg // N_DEV, cg) for g in mix_land[2 + 2 * n_b:]])

    small_shapes = [(depth, d), (depth, d), (depth, d), (d,), (N_A_LAYERS, d), k_gain.shape, q_gain.shape, rel_bias.shape]
    small_part = _pack_small(d, [jnp.concatenate(g_norm[0]), jnp.concatenate(g_norm[1]), jnp.concatenate(g_norm[2]),
                                 g_kvnorm, jnp.concatenate(g_pscale), g_kgain, jnp.concatenate(g_qgain), jnp.stack(g_bias)])
    rows_small = small_part.shape[0]
    small_all = gather_rows([small_part], 0, "gather_small")[0].reshape(N_DEV, rows_small, d)
    small_g = sum_slots(small_all, "sum_small")
    grad_ffn1_norm, grad_mix_norm, grad_ffn2_norm, grad_kv_norm, g_pscale_full, grad_k_gain, grad_q_gain, grad_rel_bias = \
        _unpack_small(small_g, small_shapes)
    grad_pool_scale = lax.dynamic_slice_in_dim(g_pscale_full, me * ds_, ds_, axis=1)

    def packed(ps):
        return _pack_small(d, [ps[0], ps[1], ps[2], ps[3], jnp.zeros((N_A_LAYERS, d), F32), ps[4], ps[5], ps[6]])

    small_w = (ffn1_norm, mix_norm, ffn2_norm, kv_norm, k_gain, q_gain, rel_bias)
    small_m = (m_ffn1_norm, m_mix_norm, m_ffn2_norm, m_kv_norm, m_k_gain, m_q_gain, m_rel_bias)
    small_v = (v_ffn1_norm, v_mix_norm, v_ffn2_norm, v_kv_norm, v_k_gain, v_q_gain, v_rel_bias)
    small_out = [_unpack_small(z, small_shapes) for z in adamw(packed(small_w), packed(small_m), packed(small_v), small_g, "adamw_small")]

    def ffn_grad(s, j):
        g = jnp.stack([g_ffn[s, l][j] for l in range(depth)])
        return g.transpose(0, 2, 1) if j < 2 else g

    grads = {
        "ffn1_w_gate": ffn_grad(0, 0), "ffn1_w_up": ffn_grad(0, 1), "ffn1_w_down": ffn_grad(0, 2),
        "ffn2_w_gate": ffn_grad(1, 0), "ffn2_w_up": ffn_grad(1, 1), "ffn2_w_down": ffn_grad(1, 2),
        "pool_w": grad_pool_w, "pool_scale": grad_pool_scale, "w_k": grad_w_k, "w_v": grad_w_v,
        "w_q": grad_w_q, "w_o": grad_w_o,
    }
    given = dict(
        ffn1_w_gate=(ffn1_w_gate, m_ffn1_w_gate, v_ffn1_w_gate), ffn1_w_up=(ffn1_w_up, m_ffn1_w_up, v_ffn1_w_up),
        ffn1_w_down=(ffn1_w_down, m_ffn1_w_down, v_ffn1_w_down), ffn2_w_gate=(ffn2_w_gate, m_ffn2_w_gate, v_ffn2_w_gate),
        ffn2_w_up=(ffn2_w_up, m_ffn2_w_up, v_ffn2_w_up), ffn2_w_down=(ffn2_w_down, m_ffn2_w_down, v_ffn2_w_down),
        pool_w=(pool_w, m_pool_w, v_pool_w), pool_scale=(pool_scale, m_pool_scale, v_pool_scale),
        w_k=(w_k, m_w_k, v_w_k), w_v=(w_v, m_w_v, v_w_v), w_q=(w_q, m_w_q, v_w_q), w_o=(w_o, m_w_o, v_w_o),
    )
    upd = {n: _adamw_nd(*given[n], grads[n]) for n in grads}
    small_names = ("ffn1_norm", "mix_norm", "ffn2_norm", "kv_norm", None, "k_gain", "q_gain", "rel_bias")
    for j, n in enumerate(small_names):
        if n is not None:
            upd[n] = tuple(small_out[z][j] for z in range(3))
    grads.update(ffn1_norm=grad_ffn1_norm, mix_norm=grad_mix_norm, ffn2_norm=grad_ffn2_norm, kv_norm=grad_kv_norm,
                 k_gain=grad_k_gain, q_gain=grad_q_gain, rel_bias=grad_rel_bias)

    order = ("ffn1_norm", "ffn1_w_gate", "ffn1_w_up", "ffn1_w_down", "mix_norm", "ffn2_norm", "ffn2_w_gate", "ffn2_w_up",
             "ffn2_w_down", "pool_w", "pool_scale", "kv_norm", "w_k", "w_v", "k_gain", "w_q", "q_gain", "rel_bias", "w_o")
    return (loss, grad_x, *[grads[n] for n in order], *[upd[n][0] for n in order], *[upd[n][1] for n in order],
            *[upd[n][2] for n in order])
```

```python
import functools

import jax
import jax.numpy as jnp
from jax import lax
from jax.experimental import pallas as pl
from jax.experimental.pallas import tpu as pltpu

F32 = jnp.float32
BF16 = jnp.bfloat16
MESH = pl.DeviceIdType.MESH

EPS = 1e-6
NEG_INF = -1e30
CHUNK = 64
LEFT = 512
REL_MAX = 128
N_REL = (CHUNK - 1) + REL_MAX + 1
POOL_WINDOWS = (2, 4, 8, 16)
N_A_LAYERS = 2
HEAD_DIM = 128
HALO = 16

ADAM_LR, ADAM_B1, ADAM_B2, ADAM_EPS, ADAM_WD, ADAM_STEP = 0.001, 0.9, 0.999, 1e-08, 0.01, 10

N_DEV = 8
AXES = ("x", "y", "c")
VMEM_LIMIT_BYTES = 56 * 1024 * 1024
TM = 512
TF = 512
BQ = 256
ROW_CHUNK = 128


def _cparams(*sem):
    return pltpu.CompilerParams(dimension_semantics=sem, vmem_limit_bytes=VMEM_LIMIT_BYTES)


def _nn(a, b):
    return jnp.dot(a, b, preferred_element_type=F32)


def _nt(a, b):
    return lax.dot_general(a, b, (((1,), (1,)), ((), ())), preferred_element_type=F32)


def _tn(a, b):
    return lax.dot_general(a, b, (((0,), (0,)), ((), ())), preferred_element_type=F32)


def _rms_fwd(x, g):
    r = lax.rsqrt(jnp.mean(x * x, axis=-1, keepdims=True) + EPS)
    return x * r * g


def _rms_bwd(x, g, dh):
    r = lax.rsqrt(jnp.mean(x * x, axis=-1, keepdims=True) + EPS)
    xn = x * r
    dxn = dh * g
    dx = r * (dxn - xn * jnp.mean(dxn * xn, axis=-1, keepdims=True))
    return dx, jnp.sum(dh * xn, axis=0, keepdims=True)


def _silu_parts(g):
    sg = jax.nn.sigmoid(g)
    return g * sg, sg * (1.0 + g * (1.0 - sg))


def _row_tile(rows, dtype=F32, cap=512):
    mult = 8 * 4 // jnp.dtype(dtype).itemsize
    for t in range(min(rows, cap), mult - 1, -1):
        if rows % t == 0 and t % mult == 0:
            return t
    return rows


_PEER_FLIPS = ((0, 0, 1), (1, 0, 0), (0, 1, 0), (1, 0, 1), (0, 1, 1), (1, 1, 0), (1, 1, 1))


def _me_and_peers():
    x, y, c = (lax.axis_index(a) for a in AXES)
    peers = []
    for fx, fy, fc in _PEER_FLIPS:
        px, py, pc = (1 - x if fx else x), (1 - y if fy else y), (1 - c if fc else c)
        peers.append(((px, py, pc), 4 * px + 2 * py + pc))
    return 4 * x + 2 * y + c, peers


class _Exchange:
    def __init__(self, kind, arrs, rows=None, pad_rows=0):
        self.gather, self.n, self.pad = kind == "gather", len(arrs), pad_rows
        self.rows = [a.shape[0] if self.gather else rows for a in arrs]
        zeros = [jnp.zeros((pad_rows, a.shape[1]), a.dtype) for a in arrs] if pad_rows else []
        self.operands = list(arrs) + zeros
        if self.gather:
            self.out_shape = [jax.ShapeDtypeStruct((N_DEV * a.shape[0] + pad_rows, a.shape[1]), a.dtype) for a in arrs]
        else:
            self.out_shape = [jax.ShapeDtypeStruct((N_DEV, rows, a.shape[1]), a.dtype) for a in arrs]
        n_remote = self.n * len(_PEER_FLIPS)
        self.scratch = [pltpu.SemaphoreType.DMA((n_remote,)), pltpu.SemaphoreType.DMA((n_remote,)),
                        pltpu.SemaphoreType.DMA((2 * self.n,))]

    def _copies(self, ins, outs, sems):
        send, recv, loc = sems
        me, peers = _me_and_peers()

        def blk(ref, idx, r):
            return ref.at[pl.ds(pl.multiple_of(idx * r, 8), r)]

        local, sends, recvs = [], [], []
        for a in range(self.n):
            r = self.rows[a]
            if self.gather:
                local.append(pltpu.make_async_copy(ins[a], blk(outs[a], me, r), loc.at[2 * a]))
                if self.pad:
                    local.append(pltpu.make_async_copy(ins[self.n + a], outs[a].at[pl.ds(N_DEV * r, self.pad)],
                                                       loc.at[2 * a + 1]))
            else:
                local.append(pltpu.make_async_copy(blk(ins[a], me, r), outs[a].at[me], loc.at[2 * a]))
        for p, (dev, idx) in enumerate(peers):
            for a in range(self.n):
                r = self.rows[a]
                k = a * len(_PEER_FLIPS) + p
                if self.gather:
                    src, there, here = ins[a], blk(outs[a], me, r), blk(outs[a], idx, r)
                else:
                    src, there, here = blk(ins[a], idx, r), outs[a].at[me], outs[a].at[idx]
                sends.append(pltpu.make_async_remote_copy(src, there, send.at[k], recv.at[k], device_id=dev,
                                                          device_id_type=MESH))
                recvs.append(pltpu.make_async_remote_copy(src, here, send.at[k], recv.at[k], device_id=dev,
                                                          device_id_type=MESH))
        return local, sends, recvs

    def start(self, ins, outs, sems):
        local, sends, _ = self._copies(ins, outs, sems)
        for cp in local + sends:
            cp.start()

    def wait(self, ins, outs, sems):
        local, sends, recvs = self._copies(ins, outs, sems)
        for cp in local:
            cp.wait()
        for cp in sends:
            cp.wait_send()
        for cp in recvs:
            cp.wait_recv()


def _host_call(body, operands, *, grid, in_specs, out_specs, out_shape, scratch_shapes=(), semantics, name, carry=()):
    n_in, n_out, n_scr = len(in_specs), len(out_specs), len(scratch_shapes)

    def wrapped(*refs):
        pos = [0]

        def take(k):
            pos[0] += k
            return refs[pos[0] - k:pos[0]]

        ins, c_ins = take(n_in), [take(len(e.operands)) for e in carry]
        outs, c_outs = take(n_out), [take(e.n) for e in carry]
        scr, c_scr = take(n_scr), [take(len(e.scratch)) for e in carry]
        ids = [pl.program_id(a) for a in range(len(grid))]
        if carry:
            @pl.when(functools.reduce(jnp.logical_and, [i == 0 for i in ids]))
            def _():
                for e, ci, co, cs in zip(carry, c_ins, c_outs, c_scr):
                    e.start(ci, co, cs)

        body(*ins, *outs, *scr)
        if carry:
            @pl.when(functools.reduce(jnp.logical_and, [i == g - 1 for i, g in zip(ids, grid)]))
            def _():
                for e, ci, co, cs in zip(carry, c_ins, c_outs, c_scr):
                    e.wait(ci, co, cs)

    any_spec = pl.BlockSpec(memory_space=pl.ANY)
    res = pl.pallas_call(
        wrapped, grid=grid,
        in_specs=list(in_specs) + [any_spec] * sum(len(e.operands) for e in carry),
        out_specs=list(out_specs) + [any_spec] * sum(e.n for e in carry),
        out_shape=list(out_shape) + [s for e in carry for s in e.out_shape],
        scratch_shapes=list(scratch_shapes) + [s for e in carry for s in e.scratch],
        compiler_params=_cparams(*(["arbitrary"] * len(grid) if carry else semantics)), name=name,
    )(*operands, *[o for e in carry for o in e.operands])
    host, rest, carried = res[:n_out], res[n_out:], []
    for e in carry:
        carried.append(rest[:e.n])
        rest = rest[e.n:]
    return host, carried


def run_exchanges(exchanges, name):
    def body(*refs):
        pos = [0]

        def take(k):
            pos[0] += k
            return refs[pos[0] - k:pos[0]]

        c_ins = [take(len(e.operands)) for e in exchanges]
        c_outs = [take(e.n) for e in exchanges]
        c_scr = [take(len(e.scratch)) for e in exchanges]
        for e, ci, co, cs in zip(exchanges, c_ins, c_outs, c_scr):
            e.start(ci, co, cs)
        for e, ci, co, cs in zip(exchanges, c_ins, c_outs, c_scr):
            e.wait(ci, co, cs)

    any_spec = pl.BlockSpec(memory_space=pl.ANY)
    res = pl.pallas_call(
        body,
        out_shape=[s for e in exchanges for s in e.out_shape],
        in_specs=[any_spec] * sum(len(e.operands) for e in exchanges),
        out_specs=[any_spec] * sum(e.n for e in exchanges),
        scratch_shapes=[s for e in exchanges for s in e.scratch],
        name=name,
    )(*[o for e in exchanges for o in e.operands])
    out = []
    for e in exchanges:
        out.append(res[:e.n])
        res = res[e.n:]
    return out


def sum_slots(land, name="sum_slots"):
    _, rows, cols = land.shape
    tr = _row_tile(rows, land.dtype)

    def body(l_ref, o_ref):
        acc = l_ref[0].astype(F32)
        for s in range(1, N_DEV):
            acc = acc + l_ref[s].astype(F32)
        o_ref[...] = acc

    return pl.pallas_call(
        body, grid=(rows // tr,),
        in_specs=[pl.BlockSpec((N_DEV, tr, cols), lambda i: (0, i, 0))],
        out_specs=pl.BlockSpec((tr, cols), lambda i: (i, 0)),
        out_shape=jax.ShapeDtypeStruct((rows, cols), F32),
        compiler_params=_cparams("parallel"), name=name,
    )(land)


def ffn_fwd(x, gain, wg_t, wu_t, wd, carry=()):
    t, d = x.shape
    fp = wd.shape[0]
    n_t, n_f = t // TM, fp // TF

    def body(x_ref, g_ref, wg_ref, wu_ref, wd_ref, xo_ref, hn_ref, gate_ref, up_ref, acc_ref):
        k = pl.program_id(1)

        @pl.when(k == 0)
        def _():
            hn_ref[...] = _rms_fwd(x_ref[...], g_ref[...]).astype(BF16)
            acc_ref[...] = jnp.zeros_like(acc_ref)

        h = hn_ref[...]
        g = _nt(h, wg_ref[...])
        u = _nt(h, wu_ref[...])
        gate_ref[...] = g.astype(BF16)
        up_ref[...] = u.astype(BF16)
        act = (g * jax.nn.sigmoid(g) * u).astype(BF16)
        acc_ref[...] += _nn(act, wd_ref[...])

        @pl.when(k == n_f - 1)
        def _():
            xo_ref[...] = x_ref[...] + 0.5 * acc_ref[...]

    tok = pl.BlockSpec((TM, d), lambda i, k: (i, 0))
    wsp = pl.BlockSpec((TF, d), lambda i, k: (k, 0))
    hid = pl.BlockSpec((TM, TF), lambda i, k: (i, k))
    return _host_call(
        body, (x, gain, wg_t, wu_t, wd), grid=(n_t, n_f),
        in_specs=[tok, pl.BlockSpec((1, d), lambda i, k: (0, 0)), wsp, wsp, wsp],
        out_specs=[tok, tok, hid, hid],
        out_shape=[jax.ShapeDtypeStruct((t, d), F32), jax.ShapeDtypeStruct((t, d), BF16),
                   jax.ShapeDtypeStruct((t, fp), BF16), jax.ShapeDtypeStruct((t, fp), BF16)],
        scratch_shapes=[pltpu.VMEM((TM, d), F32)],
        semantics=("parallel", "arbitrary"), name="ffn_fwd", carry=carry,
    )


def ffn_bwd_x(dxo, x, gain, gate, up, wg_t, wu_t, wd, carry=()):
    t, d = x.shape
    fp = wd.shape[0]
    n_t, n_f = t // TM, fp // TF

    def body(dxo_ref, x_ref, g_ref, gate_ref, up_ref, wg_ref, wu_ref, wd_ref,
             dxi_ref, dgate_ref, dup_ref, dy_ref, dgain_ref, acc_ref):
        i, k = pl.program_id(0), pl.program_id(1)

        @pl.when(k == 0)
        def _():
            dy_ref[...] = (0.5 * dxo_ref[...]).astype(BF16)
            acc_ref[...] = jnp.zeros_like(acc_ref)

        @pl.when(jnp.logical_and(i == 0, k == 0))
        def _():
            dgain_ref[...] = jnp.zeros_like(dgain_ref)

        dact = _nt(dy_ref[...], wd_ref[...])
        silu, dsilu = _silu_parts(gate_ref[...].astype(F32))
        dgate = (dact * up_ref[...].astype(F32) * dsilu).astype(BF16)
        dup = (dact * silu).astype(BF16)
        dgate_ref[...] = dgate
        dup_ref[...] = dup
        acc_ref[...] += _nn(dgate, wg_ref[...]) + _nn(dup, wu_ref[...])

        @pl.when(k == n_f - 1)
        def _():
            for c in range(TM // ROW_CHUNK):
                rs = slice(c * ROW_CHUNK, (c + 1) * ROW_CHUNK)
                dx, dg = _rms_bwd(x_ref[rs, :], g_ref[...], acc_ref[rs, :])
                dxi_ref[rs, :] = dxo_ref[rs, :] + dx
                dgain_ref[...] += dg

    tok = pl.BlockSpec((TM, d), lambda i, k: (i, 0))
    tok_once = pl.BlockSpec((TM, d), lambda i, k: (i, 0), pipeline_mode=pl.Buffered(1))
    row = pl.BlockSpec((1, d), lambda i, k: (0, 0))
    wsp = pl.BlockSpec((TF, d), lambda i, k: (k, 0))
    hid = pl.BlockSpec((TM, TF), lambda i, k: (i, k))
    return _host_call(
        body, (dxo, x, gain, gate, up, wg_t, wu_t, wd), grid=(n_t, n_f),
        in_specs=[tok_once, tok_once, row, hid, hid, wsp, wsp, wsp],
        out_specs=[tok, hid, hid, tok, row],
        out_shape=[jax.ShapeDtypeStruct((t, d), F32), jax.ShapeDtypeStruct((t, fp), BF16),
                   jax.ShapeDtypeStruct((t, fp), BF16), jax.ShapeDtypeStruct((t, d), BF16),
                   jax.ShapeDtypeStruct((1, d), F32)],
        scratch_shapes=[pltpu.VMEM((TM, d), F32)],
        semantics=("arbitrary", "arbitrary"), name="ffn_bwd_x", carry=carry,
    )


def ffn_bwd_w(hn, dy, gate, up, dgate, dup):
    t, d = hn.shape
    fp = gate.shape[1]
    n_t, n_f = t // TM, fp // TF

    def body(hn_ref, dy_ref, gate_ref, up_ref, dgate_ref, dup_ref, dwg_ref, dwu_ref, dwd_ref, ag_ref, au_ref, ad_ref):
        i = pl.program_id(1)

        @pl.when(i == 0)
        def _():
            ag_ref[...] = jnp.zeros_like(ag_ref)
            au_ref[...] = jnp.zeros_like(au_ref)
            ad_ref[...] = jnp.zeros_like(ad_ref)

        h = hn_ref[...]
        ag_ref[...] += _tn(dgate_ref[...], h)
        au_ref[...] += _tn(dup_ref[...], h)
        silu, _ = _silu_parts(gate_ref[...].astype(F32))
        act = (silu * up_ref[...].astype(F32)).astype(BF16)
        ad_ref[...] += _tn(act, dy_ref[...])

        @pl.when(i == n_t - 1)
        def _():
            dwg_ref[...] = ag_ref[...].astype(BF16)
            dwu_ref[...] = au_ref[...].astype(BF16)
            dwd_ref[...] = ad_ref[...].astype(BF16)

    tok = pl.BlockSpec((TM, d), lambda k, i: (i, 0))
    hid = pl.BlockSpec((TM, TF), lambda k, i: (i, k))
    wsp = pl.BlockSpec((TF, d), lambda k, i: (k, 0))
    return pl.pallas_call(
        body, grid=(n_f, n_t),
        in_specs=[tok, tok, hid, hid, hid, hid],
        out_specs=[wsp, wsp, wsp],
        out_shape=[jax.ShapeDtypeStruct((fp, d), BF16)] * 3,
        scratch_shapes=[pltpu.VMEM((TF, d), F32)] * 3,
        compiler_params=_cparams("parallel", "arbitrary"), name="ffn_bwd_w",
    )(hn, dy, gate, up, dgate, dup)


def rms_fwd(x, gain):
    t, d = x.shape

    def body(x_ref, g_ref, o_ref):
        o_ref[...] = _rms_fwd(x_ref[...], g_ref[...]).astype(BF16)

    tok = pl.BlockSpec((TM, d), lambda i: (i, 0))
    return pl.pallas_call(
        body, grid=(t // TM,), in_specs=[tok, pl.BlockSpec((1, d), lambda i: (0, 0))], out_specs=tok,
        out_shape=jax.ShapeDtypeStruct((t, d), BF16), compiler_params=_cparams("parallel"), name="rms_fwd",
    )(x, gain)


def rms_bwd(x, gain, dh, dx_in):
    t, d = x.shape

    def body(x_ref, g_ref, dh_ref, dxi_ref, dxo_ref, dg_ref):
        i = pl.program_id(0)

        @pl.when(i == 0)
        def _():
            dg_ref[...] = jnp.zeros_like(dg_ref)

        dx, dg = _rms_bwd(x_ref[...], g_ref[...], dh_ref[...])
        dxo_ref[...] = dxi_ref[...] + dx
        dg_ref[...] += dg

    tok = pl.BlockSpec((TM, d), lambda i: (i, 0))
    row = pl.BlockSpec((1, d), lambda i: (0, 0))
    return pl.pallas_call(
        body, grid=(t // TM,), in_specs=[tok, row, tok, tok], out_specs=[tok, row],
        out_shape=[jax.ShapeDtypeStruct((t, d), F32), jax.ShapeDtypeStruct((1, d), F32)],
        compiler_params=_cparams("arbitrary"), name="rms_bwd",
    )(x, gain, dh, dx_in)


def proj_nn(a, w, *, head_gain=None, residual=None, out_dtype=F32, name="proj_nn"):
    t, kd = a.shape
    n = w.shape[1]
    tn = min(512, n)

    def body(*refs):
        a_ref, w_ref = refs[:2]
        acc = _nn(a_ref[...], w_ref[...])
        if head_gain is not None:
            g_ref, o_ref, on_ref = refs[2:]
            o_ref[...] = acc
            for j in range(tn // HEAD_DIM):
                sl = slice(j * HEAD_DIM, (j + 1) * HEAD_DIM)
                on_ref[:, sl] = _rms_fwd(acc[:, sl], g_ref[...]).astype(BF16)
        elif residual is not None:
            r_ref, o_ref = refs[2:]
            o_ref[...] = r_ref[...] + acc
        else:
            refs[2][...] = acc.astype(out_dtype)

    a_spec = pl.BlockSpec((TM, kd), lambda i, j: (i, 0))
    w_spec = pl.BlockSpec((kd, tn), lambda i, j: (0, j))
    o_spec = pl.BlockSpec((TM, tn), lambda i, j: (i, j))
    ins, in_specs = [a, w], [a_spec, w_spec]
    if head_gain is not None:
        ins.append(head_gain)
        in_specs.append(pl.BlockSpec((1, HEAD_DIM), lambda i, j: (0, 0)))
        out_specs = [o_spec, o_spec]
        out_shape = [jax.ShapeDtypeStruct((t, n), F32), jax.ShapeDtypeStruct((t, n), BF16)]
    elif residual is not None:
        ins.append(residual)
        in_specs.append(o_spec)
        out_specs, out_shape = o_spec, jax.ShapeDtypeStruct((t, n), F32)
    else:
        out_specs, out_shape = o_spec, jax.ShapeDtypeStruct((t, n), out_dtype)
    return pl.pallas_call(
        body, grid=(t // TM, n // tn), in_specs=in_specs, out_specs=out_specs, out_shape=out_shape,
        compiler_params=_cparams("parallel", "parallel"), name=name,
    )(*ins)


def proj_nt(a, w, *, addend=None, out_dtype=F32, name="proj_nt"):
    t, n = a.shape
    kd = w.shape[0]
    tk = min(512, kd)

    def body(*refs):
        a_ref, w_ref = refs[:2]
        acc = _nt(a_ref[...], w_ref[...])
        if addend is not None:
            acc = acc + refs[2][...]
        refs[-1][...] = acc.astype(out_dtype)

    o_spec = pl.BlockSpec((TM, tk), lambda i, j: (i, j))
    ins = [a, w] + ([addend] if addend is not None else [])
    in_specs = [pl.BlockSpec((TM, n), lambda i, j: (i, 0)), pl.BlockSpec((tk, n), lambda i, j: (j, 0))]
    if addend is not None:
        in_specs.append(o_spec)
    return pl.pallas_call(
        body, grid=(t // TM, kd // tk), in_specs=in_specs, out_specs=o_spec,
        out_shape=jax.ShapeDtypeStruct((t, kd), out_dtype),
        compiler_params=_cparams("parallel", "parallel"), name=name,
    )(*ins)


def grad_tn(a, b, name="grad_tn"):
    t, m = a.shape
    n = b.shape[1]
    tmo, tno = min(1024, m), min(1024, n)
    n_t = t // TM

    def body(a_ref, b_ref, o_ref, acc_ref):
        k = pl.program_id(2)

        @pl.when(k == 0)
        def _():
            acc_ref[...] = jnp.zeros_like(acc_ref)

        acc_ref[...] += _tn(a_ref[...], b_ref[...])

        @pl.when(k == n_t - 1)
        def _():
            o_ref[...] = acc_ref[...].astype(BF16)

    return pl.pallas_call(
        body, grid=(m // tmo, n // tno, n_t),
        in_specs=[pl.BlockSpec((TM, tmo), lambda i, j, k: (k, i)), pl.BlockSpec((TM, tno), lambda i, j, k: (k, j))],
        out_specs=pl.BlockSpec((tmo, tno), lambda i, j, k: (i, j)),
        out_shape=jax.ShapeDtypeStruct((m, n), BF16),
        scratch_shapes=[pltpu.VMEM((tmo, tno), F32)],
        compiler_params=_cparams("parallel", "parallel", "arbitrary"), name=name,
    )(a, b)


def head_rms_bwd(dy, pre, gain):
    t, d = pre.shape

    def body(dy_ref, p_ref, g_ref, o_ref, dg_ref):
        i = pl.program_id(0)

        @pl.when(i == 0)
        def _():
            dg_ref[...] = jnp.zeros_like(dg_ref)

        dg_acc = jnp.zeros((1, HEAD_DIM), F32)
        for j in range(d // HEAD_DIM):
            sl = slice(j * HEAD_DIM, (j + 1) * HEAD_DIM)
            dx, dg = _rms_bwd(p_ref[:, sl], g_ref[...], dy_ref[:, sl])
            o_ref[:, sl] = dx.astype(BF16)
            dg_acc = dg_acc + dg
        dg_ref[...] += dg_acc

    tok = pl.BlockSpec((TM, d), lambda i: (i, 0))
    row = pl.BlockSpec((1, HEAD_DIM), lambda i: (0, 0))
    return pl.pallas_call(
        body, grid=(t // TM,), in_specs=[tok, tok, row], out_specs=[tok, row],
        out_shape=[jax.ShapeDtypeStruct((t, d), BF16), jax.ShapeDtypeStruct((1, HEAD_DIM), F32)],
        compiler_params=_cparams("arbitrary"), name="head_rms_bwd",
    )(dy, pre, gain)


def band_bias(table):
    h = table.shape[0]
    far = jnp.broadcast_to(table[:, N_REL - 1:], (h, LEFT - CHUNK))
    w = jnp.concatenate([far, jnp.flip(table[:, :N_REL - 1], axis=1)], axis=1)
    band = jnp.stack([w[:, CHUNK - 1 - r:CHUNK - 1 - r + LEFT + CHUNK] for r in range(CHUNK)], axis=1)
    out = jnp.full((h, BQ, BQ + LEFT), NEG_INF, F32)
    for a in range(BQ // CHUNK):
        out = out.at[:, a * CHUNK:(a + 1) * CHUNK, a * CHUNK:a * CHUNK + LEFT + CHUNK].set(band)
    return out


def _attn_probs(q_ref, k, b_ref, start):
    w = k.shape[0]
    s = _nt(q_ref[...], k) * (HEAD_DIM ** -0.5) + b_ref[...]
    kpos = start + lax.broadcasted_iota(jnp.int32, (1, w), 1)
    s = jnp.where(kpos >= LEFT, s, NEG_INF)
    e = jnp.exp(s - jnp.max(s, axis=-1, keepdims=True))
    return e / jnp.sum(e, axis=-1, keepdims=True)


def attn_fwd(qn, k_pad, v_pad, bias):
    t, d = qn.shape
    tp = k_pad.shape[0]
    w = BQ + LEFT

    def body(q_ref, k_ref, v_ref, b_ref, o_ref):
        start = pl.multiple_of(pl.program_id(1) * BQ, BQ)
        p = _attn_probs(q_ref, k_ref[pl.ds(start, w), :], b_ref, start)
        o_ref[...] = _nn(p.astype(BF16), v_ref[pl.ds(start, w), :]).astype(BF16)

    q_spec = pl.BlockSpec((BQ, HEAD_DIM), lambda h, i: (i, h))
    kv_spec = pl.BlockSpec((tp, HEAD_DIM), lambda h, i: (0, h))
    return pl.pallas_call(
        body, grid=(d // HEAD_DIM, t // BQ),
        in_specs=[q_spec, kv_spec, kv_spec, pl.BlockSpec((None, BQ, w), lambda h, i: (h, 0, 0))],
        out_specs=q_spec, out_shape=jax.ShapeDtypeStruct((t, d), BF16),
        compiler_params=_cparams("parallel", "parallel"), name="attn_fwd",
    )(qn, k_pad, v_pad, bias)


def attn_bwd(qn, k_pad, v_pad, bias, do):
    t, d = qn.shape
    tp = k_pad.shape[0]
    w = BQ + LEFT
    scale = HEAD_DIM ** -0.5

    def body(q_ref, k_ref, v_ref, b_ref, do_ref, dq_ref, dk_ref, dv_ref, db_ref):
        i = pl.program_id(1)
        start = pl.multiple_of(i * BQ, BQ)

        @pl.when(i == 0)
        def _():
            dk_ref[...] = jnp.zeros_like(dk_ref)
            dv_ref[...] = jnp.zeros_like(dv_ref)
            db_ref[...] = jnp.zeros_like(db_ref)

        k = k_ref[pl.ds(start, w), :]
        v = v_ref[pl.ds(start, w), :]
        p = _attn_probs(q_ref, k, b_ref, start)
        dp = _nt(do_ref[...], v)
        ds = p * (dp - jnp.sum(p * dp, axis=-1, keepdims=True))
        db_ref[...] += ds
        dsb = ds.astype(BF16)
        dq_ref[...] = _nn(dsb, k) * scale
        dk_ref[pl.ds(start, w), :] += _tn(dsb, q_ref[...]) * scale
        dv_ref[pl.ds(start, w), :] += _tn(p.astype(BF16), do_ref[...])

    q_spec = pl.BlockSpec((BQ, HEAD_DIM), lambda h, i: (i, h))
    kv_spec = pl.BlockSpec((tp, HEAD_DIM), lambda h, i: (0, h))
    b_spec = pl.BlockSpec((None, BQ, w), lambda h, i: (h, 0, 0))
    return pl.pallas_call(
        body, grid=(d // HEAD_DIM, t // BQ),
        in_specs=[q_spec, kv_spec, kv_spec, b_spec, q_spec],
        out_specs=[q_spec, kv_spec, kv_spec, b_spec],
        out_shape=[jax.ShapeDtypeStruct((t, d), F32), jax.ShapeDtypeStruct((tp, d), F32),
                   jax.ShapeDtypeStruct((tp, d), F32), jax.ShapeDtypeStruct(bias.shape, F32)],
        compiler_params=_cparams("parallel", "arbitrary"), name="attn_bwd",
    )(qn, k_pad, v_pad, bias, do)


def _pool_diff(ext_ref, h, t0, g, cg):
    tm = h.shape[0]
    cs = slice(g * cg, (g + 1) * cg)
    win = POOL_WINDOWS[g]
    s = ext_ref[HALO:HALO + tm, cs]
    for j in range(1, win):
        s = s + ext_ref[HALO - j:HALO - j + tm, cs]
    cnt = jnp.minimum(t0 + lax.broadcasted_iota(jnp.int32, (tm, 1), 0) + 1, win).astype(F32)
    return (s / cnt - h[:, cs]).astype(BF16), cnt


def pool_fwd(x, gain, w, scale):
    t, d = x.shape
    n_g = len(POOL_WINDOWS)
    cg = d // n_g

    def body(x_ref, g_ref, w_ref, s_ref, o_ref, ext_ref):
        i = pl.program_id(0)
        h = _rms_fwd(x_ref[...], g_ref[...])

        @pl.when(i == 0)
        def _():
            ext_ref[0:HALO, :] = jnp.zeros((HALO, d), F32)

        @pl.when(i > 0)
        def _():
            ext_ref[0:HALO, :] = ext_ref[TM:TM + HALO, :]

        ext_ref[HALO:HALO + TM, :] = h
        for g in range(n_g):
            cs = slice(g * cg, (g + 1) * cg)
            diff, _ = _pool_diff(ext_ref, h, i * TM, g, cg)
            o_ref[:, cs] = x_ref[:, cs] + _nn(diff, w_ref[g]) * s_ref[:, cs]

    tok = pl.BlockSpec((TM, d), lambda i: (i, 0))
    row = pl.BlockSpec((1, d), lambda i: (0, 0))
    return pl.pallas_call(
        body, grid=(t // TM,),
        in_specs=[tok, row, pl.BlockSpec((n_g, cg, cg), lambda i: (0, 0, 0)), row],
        out_specs=tok, out_shape=jax.ShapeDtypeStruct((t, d), F32),
        scratch_shapes=[pltpu.VMEM((HALO + TM, d), F32)],
        compiler_params=_cparams("arbitrary"), name="pool_fwd",
    )(x, gain, w, scale)


def pool_bwd(x, dxo, gain, w, scale):
    t, d = x.shape
    n_g = len(POOL_WINDOWS)
    cg = d // n_g
    n_t = t // TM

    def body(x_ref, xh_ref, dxo_ref, g_ref, w_ref, s_ref, dxi_ref, dw_ref, ds_ref, dg_ref, exth_ref, extq_ref, dh_ref):
        i = pl.program_id(0)
        ti = n_t - 1 - i

        @pl.when(i == 0)
        def _():
            dw_ref[...] = jnp.zeros_like(dw_ref)
            ds_ref[...] = jnp.zeros_like(ds_ref)
            dg_ref[...] = jnp.zeros_like(dg_ref)
            extq_ref[TM:TM + HALO, :] = jnp.zeros((HALO, d), F32)

        @pl.when(i > 0)
        def _():
            extq_ref[TM:TM + HALO, :] = extq_ref[0:HALO, :]

        xv = x_ref[...]
        h = _rms_fwd(xv, g_ref[...])
        exth_ref[0:HALO, :] = jnp.where(ti > 0, _rms_fwd(xh_ref[...], g_ref[...]), 0.0)
        exth_ref[HALO:HALO + TM, :] = h
        for g in range(n_g):
            cs = slice(g * cg, (g + 1) * cg)
            diff, cnt = _pool_diff(exth_ref, h, ti * TM, g, cg)
            dxo_g = dxo_ref[:, cs]
            ds_ref[:, cs] += jnp.sum(dxo_g * _nn(diff, w_ref[g]), axis=0, keepdims=True)
            dy = (dxo_g * s_ref[:, cs]).astype(BF16)
            dw_ref[g] += _tn(diff, dy)
            ddiff = _nt(dy, w_ref[g])
            extq_ref[0:TM, cs] = ddiff / cnt
            acc = -ddiff
            for j in range(POOL_WINDOWS[g]):
                acc = acc + extq_ref[j:j + TM, cs]
            dh_ref[:, cs] = acc
        dx, dg = _rms_bwd(xv, g_ref[...], dh_ref[...])
        dxi_ref[...] = dxo_ref[...] + dx
        dg_ref[...] += dg

    tok = pl.BlockSpec((TM, d), lambda i: (n_t - 1 - i, 0))
    halo = pl.BlockSpec((HALO, d), lambda i: (jnp.maximum((n_t - 1 - i) * (TM // HALO) - 1, 0), 0))
    row = pl.BlockSpec((1, d), lambda i: (0, 0))
    wsp = pl.BlockSpec((n_g, cg, cg), lambda i: (0, 0, 0))
    return pl.pallas_call(
        body, grid=(n_t,),
        in_specs=[tok, halo, tok, row, wsp, row],
        out_specs=[tok, wsp, row, row],
        out_shape=[jax.ShapeDtypeStruct((t, d), F32), jax.ShapeDtypeStruct((n_g, cg, cg), F32),
                   jax.ShapeDtypeStruct((1, d), F32), jax.ShapeDtypeStruct((1, d), F32)],
        scratch_shapes=[pltpu.VMEM((HALO + TM, d), F32), pltpu.VMEM((TM + HALO, d), F32), pltpu.VMEM((TM, d), F32)],
        compiler_params=_cparams("arbitrary"), name="pool_bwd",
    )(x, x, dxo, gain, w, scale)


def loss_head(y, target):
    t, d = y.shape

    def body(y_ref, t_ref, sq_ref, dy_ref):
        i = pl.program_id(0)

        @pl.when(i == 0)
        def _():
            sq_ref[...] = jnp.zeros_like(sq_ref)

        err = y_ref[...] - t_ref[...]
        sq_ref[...] += jnp.sum(err * err, axis=0, keepdims=True)
        dy_ref[...] = err * (1.0 / d)

    tok = pl.BlockSpec((TM, d), lambda i: (i, 0))
    row = pl.BlockSpec((1, d), lambda i: (0, 0))
    return pl.pallas_call(
        body, grid=(t // TM,), in_specs=[tok, tok], out_specs=[row, tok],
        out_shape=[jax.ShapeDtypeStruct((1, d), F32), jax.ShapeDtypeStruct((t, d), F32)],
        compiler_params=_cparams("arbitrary"), name="loss_head",
    )(y, target)


def adamw(w, m, v, g, name="adamw"):
    rows, cols = w.shape
    tr = _row_tile(rows)

    def body(w_ref, m_ref, v_ref, g_ref, d_ref, mo_ref, vo_ref):
        gv = g_ref[...]
        mn = ADAM_B1 * m_ref[...] + (1.0 - ADAM_B1) * gv
        vn = ADAM_B2 * v_ref[...] + (1.0 - ADAM_B2) * jnp.square(gv)
        m_hat = mn / (1.0 - ADAM_B1 ** ADAM_STEP)
        v_hat = vn / (1.0 - ADAM_B2 ** ADAM_STEP)
        d_ref[...] = -ADAM_LR * (m_hat / (jnp.sqrt(v_hat) + ADAM_EPS) + ADAM_WD * w_ref[...])
        mo_ref[...] = mn
        vo_ref[...] = vn

    blk = pl.BlockSpec((tr, cols), lambda i: (i, 0))
    return pl.pallas_call(
        body, grid=(rows // tr,), in_specs=[blk] * 4, out_specs=[blk] * 3,
        out_shape=[jax.ShapeDtypeStruct((rows, cols), F32)] * 3,
        compiler_params=_cparams("parallel"), name=name,
    )(w, m, v, g)


def _adamw_nd(w, m, v, g):
    shape = w.shape
    cols = shape[-1]
    d, mn, vn = adamw(w.reshape(-1, cols), m.reshape(-1, cols), v.reshape(-1, cols), g.reshape(-1, cols))
    return d.reshape(shape), mn.reshape(shape), vn.reshape(shape)


def _pack_small(d, parts):
    flat = jnp.concatenate([p.reshape(-1).astype(F32) for p in parts])
    rows = -(-flat.shape[0] // d)
    rows = -(-rows // 8) * 8
    return jnp.pad(flat, (0, rows * d - flat.shape[0])).reshape(rows, d)


def _unpack_small(packed, shapes):
    flat = packed.reshape(-1)
    out, off = [], 0
    for s in shapes:
        n = 1
        for k in s:
            n *= k
        out.append(flat[off:off + n].reshape(s))
        off += n
    return out


def kernel(x, ffn1_norm, ffn1_w_gate, ffn1_w_up, ffn1_w_down, mix_norm, ffn2_norm, ffn2_w_gate, ffn2_w_up, ffn2_w_down, pool_w, pool_scale, kv_norm, w_k, w_v, k_gain, w_q, q_gain, rel_bias, w_o, loss_target, m_ffn1_norm, m_ffn1_w_gate, m_ffn1_w_up, m_ffn1_w_down, m_mix_norm, m_ffn2_norm, m_ffn2_w_gate, m_ffn2_w_up, m_ffn2_w_down, m_pool_w, m_pool_scale, m_kv_norm, m_w_k, m_w_v, m_k_gain, m_w_q, m_q_gain, m_rel_bias, m_w_o, v_ffn1_norm, v_ffn1_w_gate, v_ffn1_w_up, v_ffn1_w_down, v_mix_norm, v_ffn2_norm, v_ffn2_w_gate, v_ffn2_w_up, v_ffn2_w_down, v_pool_w, v_pool_scale, v_kv_norm, v_w_k, v_w_v, v_k_gain, v_w_q, v_q_gain, v_rel_bias, v_w_o):
    x0, target = x[0], loss_target[0]
    t, d = x0.shape
    depth = ffn1_norm.shape[0]
    n_b = depth - N_A_LAYERS
    fs = ffn1_w_gate.shape[-1]
    f_pad = -(-(N_DEV * fs) // TF) * TF - N_DEV * fs
    n_g = len(POOL_WINDOWS)
    cg = d // n_g
    ds_ = d // N_DEV
    me = 4 * lax.axis_index("x") + 2 * lax.axis_index("y") + lax.axis_index("c")

    ffn_in = ((ffn1_norm, ffn1_w_gate, ffn1_w_up, ffn1_w_down), (ffn2_norm, ffn2_w_gate, ffn2_w_up, ffn2_w_down))

    def ffn_gather(l, s):
        _, wg, wu, wd = ffn_in[s]
        return _Exchange("gather", [wg[l].T.astype(BF16), wu[l].T.astype(BF16), wd[l].astype(BF16)], pad_rows=f_pad)

    scale_rows = jnp.pad(pool_scale, ((0, 8 - pool_scale.shape[0]), (0, 0)))
    mix_loc = [w_k.astype(BF16), w_v.astype(BF16)]
    mix_loc += [w_q[b].astype(BF16) for b in range(n_b)] + [w_o[b].astype(BF16) for b in range(n_b)]
    mix_loc += [pool_w[l].reshape(n_g * cg // N_DEV, cg).astype(BF16) for l in range(N_A_LAYERS)]
    bias = [band_bias(rel_bias[b]) for b in range(n_b)]

    ffn_w = {(0, 0): run_exchanges([ffn_gather(0, 0)], "gather_first")[0]}
    saved = []
    xc = x0
    k_pad = v_pad = kv_saved = None
    for l in range(depth):
        rec = {"x_a": xc}
        carry = [ffn_gather(l, 1)] + ([_Exchange("gather", mix_loc + [scale_rows])] if l == 0 else [])
        (xb, rec["hn1"], rec["g1"], rec["u1"]), got = ffn_fwd(xc, ffn1_norm[l:l + 1], *ffn_w[0, l], carry=carry)
        ffn_w[1, l] = got[0]
        if l == 0:
            mix_all = got[1]
            wk_f, wv_f = mix_all[0], mix_all[1]
            wq_f, wo_f = mix_all[2:2 + n_b], mix_all[2 + n_b:2 + 2 * n_b]
            pool_f = [p.reshape(N_DEV, n_g, cg // N_DEV, cg).transpose(1, 0, 2, 3).reshape(n_g, cg, cg)
                      for p in mix_all[2 + 2 * n_b:2 + 2 * n_b + N_A_LAYERS]]
            scale_f = mix_all[-1].reshape(N_DEV, 8, ds_)[:, :N_A_LAYERS].transpose(1, 0, 2).reshape(N_A_LAYERS, d)
        rec["x_b"] = xb
        if l < N_A_LAYERS:
            xm = pool_fwd(xb, mix_norm[l:l + 1], pool_f[l], scale_f[l:l + 1])
        else:
            b = l - N_A_LAYERS
            hm = rms_fwd(xb, mix_norm[l:l + 1])
            q_pre, qn = proj_nn(hm, wq_f[b], head_gain=q_gain[b:b + 1], name="proj_q")
            o = attn_fwd(qn, k_pad, v_pad, bias[b])
            xm = proj_nn(o, wo_f[b], residual=xb, name="proj_o")
            rec.update(hm=hm, q_pre=q_pre, qn=qn, o=o)
        rec["x_c"] = xm
        carry = [ffn_gather(l + 1, 0)] if l + 1 < depth else []
        (xc, rec["hn2"], rec["g2"], rec["u2"]), got = ffn_fwd(xm, ffn2_norm[l:l + 1], *ffn_w[1, l], carry=carry)
        if carry:
            ffn_w[0, l + 1] = got[0]
        if l == N_A_LAYERS - 1:
            hk = rms_fwd(xc, kv_norm[None])
            k_pre, kn = proj_nn(hk, wk_f, head_gain=k_gain[None], name="proj_q")
            vv = proj_nn(hk, wv_f, out_dtype=BF16, name="proj_v")
            k_pad = jnp.pad(kn, ((LEFT, 0), (0, 0)))
            v_pad = jnp.pad(vv, ((LEFT, 0), (0, 0)))
            kv_saved = dict(x=xc, hk=hk, k_pre=k_pre)
        saved.append(rec)
    sq, dx = loss_head(xc, target)
    loss = lax.psum(0.5 * jnp.sum(sq) / d, AXES)

    g_ffn = {}
    g_norm = {0: [None] * depth, 1: [None] * depth, 2: [None] * depth}
    g_wq, g_wo, g_pool, g_qgain, g_bias, g_pscale = [None] * n_b, [None] * n_b, [None] * N_A_LAYERS, [None] * n_b, [None] * n_b, [None] * N_A_LAYERS
    dk_sum = dv_sum = None

    pending = []

    def ffn_backward(dx, l, s, x_in, hn, gate, up, extra=()):
        carry = [_Exchange("scatter", dws, rows=fs) for _, dws in pending] + list(extra)
        (dx, dgate, dup, dy, dgain), got = ffn_bwd_x(dx, x_in, ffn_in[s][0][l:l + 1], gate, up, *ffn_w[s, l], carry=carry)
        for (key, _), land in zip(pending, got):
            g_ffn[key] = [sum_slots(z, "sum_ffn") for z in land]
        pending[:] = [((s, l), ffn_bwd_w(hn, dy, gate, up, dgate, dup))]
        g_norm[2 * s][l] = dgain
        return dx, got[len(got) - len(extra):]

    for l in reversed(range(depth)):
        rec = saved[l]
        dx, _ = ffn_backward(dx, l, 1, rec["x_c"], rec["hn2"], rec["g2"], rec["u2"])
        if l < N_A_LAYERS:
            dx, g_pool[l], g_pscale[l], g_norm[1][l] = pool_bwd(rec["x_b"], dx, mix_norm[l:l + 1], pool_f[l], scale_f[l:l + 1])
        else:
            b = l - N_A_LAYERS
            dxb = dx.astype(BF16)
            do = proj_nt(dxb, wo_f[b], out_dtype=BF16, name="proj_do")
            g_wo[b] = grad_tn(rec["o"], dxb)
            dq, dk, dv, dbias = attn_bwd(rec["qn"], k_pad, v_pad, bias[b], do)
            dk_sum = dk if dk_sum is None else dk_sum + dk
            dv_sum = dv if dv_sum is None else dv_sum + dv
            g_bias[b] = jax.vjp(band_bias, rel_bias[b])[1](dbias)[0]
            dq_pre, g_qgain[b] = head_rms_bwd(dq, rec["q_pre"], q_gain[b:b + 1])
            g_wq[b] = grad_tn(rec["hm"], dq_pre)
            dhm = proj_nt(dq_pre, wq_f[b], name="proj_dh")
            dx, g_norm[1][l] = rms_bwd(rec["x_b"], mix_norm[l:l + 1], dhm, dx)
        extra = []
        if l == 0:
            mix_g = [g_wk, g_wv] + g_wq + g_wo
            mix_g += [g.reshape(n_g, N_DEV, cg // N_DEV, cg).transpose(1, 0, 2, 3).reshape(n_g * cg, cg).astype(BF16)
                      for g in g_pool]
            extra = [_Exchange("scatter", mix_g, rows=ds_)]
        dx, got = ffn_backward(dx, l, 0, rec["x_a"], rec["hn1"], rec["g1"], rec["u1"], extra)
        if l == 0:
            mix_land = [sum_slots(z, "sum_mix") for z in got[0]]
        if l == N_A_LAYERS:
            dk_pre, g_kgain = head_rms_bwd(dk_sum[LEFT:], kv_saved["k_pre"], k_gain[None])
            dvb = dv_sum[LEFT:].astype(BF16)
            g_wk = grad_tn(kv_saved["hk"], dk_pre)
            g_wv = grad_tn(kv_saved["hk"], dvb)
            dhk = proj_nt(dvb, wv_f, addend=proj_nt(dk_pre, wk_f, name="proj_dh"), name="proj_dh2")
            dx, g_kvnorm = rms_bwd(kv_saved["x"], kv_norm[None], dhk, dx)
    grad_x = dx[None]

    grad_w_k, grad_w_v = mix_land[0], mix_land[1]
    grad_w_q = jnp.stack(mix_land[2:2 + n_b])
    grad_w_o = jnp.stack(mix_land[2 + n_b:2 + 2 * n_b])
    grad_pool_w = jnp.stack([g.reshape(n_g, cg // N_DEV, cg) for g in mix_land[2 + 2 * n_b:]])

    small_shapes = [(depth, d), (depth, d), (depth, d), (d,), (N_A_LAYERS, d), k_gain.shape, q_gain.shape, rel_bias.shape]
    small_part = _pack_small(d, [jnp.concatenate(g_norm[0]), jnp.concatenate(g_norm[1]), jnp.concatenate(g_norm[2]),
                                 g_kvnorm, jnp.concatenate(g_pscale), g_kgain, jnp.concatenate(g_qgain), jnp.stack(g_bias)])
    rows_small = small_part.shape[0]
    (key, dws), = pending
    land, (small_all,) = run_exchanges([_Exchange("scatter", dws, rows=fs), _Exchange("gather", [small_part])], "exchange_last")
    g_ffn[key] = [sum_slots(z, "sum_ffn") for z in land]
    small_g = sum_slots(small_all.reshape(N_DEV, rows_small, d), "sum_small")
    grad_ffn1_norm, grad_mix_norm, grad_ffn2_norm, grad_kv_norm, g_pscale_full, grad_k_gain, grad_q_gain, grad_rel_bias = \
        _unpack_small(small_g, small_shapes)
    grad_pool_scale = lax.dynamic_slice_in_dim(g_pscale_full, me * ds_, ds_, axis=1)

    def packed(ps):
        return _pack_small(d, [ps[0], ps[1], ps[2], ps[3], jnp.zeros((N_A_LAYERS, d), F32), ps[4], ps[5], ps[6]])

    small_w = (ffn1_norm, mix_norm, ffn2_norm, kv_norm, k_gain, q_gain, rel_bias)
    small_m = (m_ffn1_norm, m_mix_norm, m_ffn2_norm, m_kv_norm, m_k_gain, m_q_gain, m_rel_bias)
    small_v = (v_ffn1_norm, v_mix_norm, v_ffn2_norm, v_kv_norm, v_k_gain, v_q_gain, v_rel_bias)
    small_out = [_unpack_small(z, small_shapes) for z in adamw(packed(small_w), packed(small_m), packed(small_v), small_g, "adamw_small")]

    def ffn_grad(s, j):
        g = jnp.stack([g_ffn[s, l][j] for l in range(depth)])
        return g.transpose(0, 2, 1) if j < 2 else g

    grads = {
        "ffn1_w_gate": ffn_grad(0, 0), "ffn1_w_up": ffn_grad(0, 1), "ffn1_w_down": ffn_grad(0, 2),
        "ffn2_w_gate": ffn_grad(1, 0), "ffn2_w_up": ffn_grad(1, 1), "ffn2_w_down": ffn_grad(1, 2),
        "pool_w": grad_pool_w, "pool_scale": grad_pool_scale, "w_k": grad_w_k, "w_v": grad_w_v,
        "w_q": grad_w_q, "w_o": grad_w_o,
    }
    given = dict(
        ffn1_w_gate=(ffn1_w_gate, m_ffn1_w_gate, v_ffn1_w_gate), ffn1_w_up=(ffn1_w_up, m_ffn1_w_up, v_ffn1_w_up),
        ffn1_w_down=(ffn1_w_down, m_ffn1_w_down, v_ffn1_w_down), ffn2_w_gate=(ffn2_w_gate, m_ffn2_w_gate, v_ffn2_w_gate),
        ffn2_w_up=(ffn2_w_up, m_ffn2_w_up, v_ffn2_w_up), ffn2_w_down=(ffn2_w_down, m_ffn2_w_down, v_ffn2_w_down),
        pool_w=(pool_w, m_pool_w, v_pool_w), pool_scale=(pool_scale, m_pool_scale, v_pool_scale),
        w_k=(w_k, m_w_k, v_w_k), w_v=(w_v, m_w_v, v_w_v), w_q=(w_q, m_w_q, v_w_q), w_o=(w_o, m_w_o, v_w_o),
    )
    upd = {n: _adamw_nd(*given[n], grads[n]) for n in grads}
    small_names = ("ffn1_norm", "mix_norm", "ffn2_norm", "kv_norm", None, "k_gain", "q_gain", "rel_bias")
    for j, n in enumerate(small_names):
        if n is not None:
            upd[n] = tuple(small_out[z][j] for z in range(3))
    grads.update(ffn1_norm=grad_ffn1_norm, mix_norm=grad_mix_norm, ffn2_norm=grad_ffn2_norm, kv_norm=grad_kv_norm,
                 k_gain=grad_k_gain, q_gain=grad_q_gain, rel_bias=grad_rel_bias)

    order = ("ffn1_norm", "ffn1_w_gate", "ffn1_w_up", "ffn1_w_down", "mix_norm", "ffn2_norm", "ffn2_w_gate", "ffn2_w_up",
             "ffn2_w_down", "pool_w", "pool_scale", "kv_norm", "w_k", "w_v", "k_gain", "w_q", "q_gain", "rel_bias", "w_o")
    return (loss, grad_x, *[grads[n] for n in order], *[upd[n][0] for n in order], *[upd[n][1] for n in order],
            *[upd[n][2] for n in order])
```

```python
import functools

import jax
import jax.numpy as jnp
from jax import lax
from jax.experimental import pallas as pl
from jax.experimental.pallas import tpu as pltpu

F32 = jnp.float32
BF16 = jnp.bfloat16
MESH = pl.DeviceIdType.MESH

EPS = 1e-6
NEG_INF = -1e30
CHUNK = 64
LEFT = 512
REL_MAX = 128
N_REL = (CHUNK - 1) + REL_MAX + 1
POOL_WINDOWS = (2, 4, 8, 16)
N_A_LAYERS = 2
HEAD_DIM = 128
HALO = 16

ADAM_LR, ADAM_B1, ADAM_B2, ADAM_EPS, ADAM_WD, ADAM_STEP = 0.001, 0.9, 0.999, 1e-08, 0.01, 10

N_DEV = 8
AXES = ("x", "y", "c")
VMEM_LIMIT_BYTES = 56 * 1024 * 1024
TM = 512
TM_FFN_FWD = 1024
TM_FFN_BWD_W = 1024
TK_GRAD = 2048
TF = 512
BQ = 256
QB_STEP = 4
ROW_CHUNK = 128
SUM_BLOCK_BYTES = 4 * 1024 * 1024
PROJ_CHUNK = 512
FF_CHUNK = 256


def _cparams(*sem):
    return pltpu.CompilerParams(dimension_semantics=sem, vmem_limit_bytes=VMEM_LIMIT_BYTES)


def _nn(a, b):
    return jnp.dot(a, b, preferred_element_type=F32)


def _nt(a, b):
    return lax.dot_general(a, b, (((1,), (1,)), ((), ())), preferred_element_type=F32)


def _tn(a, b):
    return lax.dot_general(a, b, (((0,), (0,)), ((), ())), preferred_element_type=F32)


def _rms_fwd(x, g):
    r = lax.rsqrt(jnp.mean(x * x, axis=-1, keepdims=True) + EPS)
    return x * r * g


def _rms_bwd(x, g, dh):
    r = lax.rsqrt(jnp.mean(x * x, axis=-1, keepdims=True) + EPS)
    xn = x * r
    dxn = dh * g
    dx = r * (dxn - xn * jnp.mean(dxn * xn, axis=-1, keepdims=True))
    return dx, jnp.sum(dh * xn, axis=0, keepdims=True)


def _silu_parts(g):
    sg = jax.nn.sigmoid(g)
    return g * sg, sg * (1.0 + g * (1.0 - sg))


def _row_tile(rows, dtype=F32, cap=512):
    mult = 8 * 4 // jnp.dtype(dtype).itemsize
    for t in range(min(rows, cap), mult - 1, -1):
        if rows % t == 0 and t % mult == 0:
            return t
    return rows


_PEER_FLIPS = ((0, 0, 1), (1, 0, 0), (0, 1, 0), (1, 0, 1), (0, 1, 1), (1, 1, 0), (1, 1, 1))


def _me_and_peers():
    x, y, c = (lax.axis_index(a) for a in AXES)
    peers = []
    for fx, fy, fc in _PEER_FLIPS:
        px, py, pc = (1 - x if fx else x), (1 - y if fy else y), (1 - c if fc else c)
        peers.append(((px, py, pc), 4 * px + 2 * py + pc))
    return 4 * x + 2 * y + c, peers


class _Exchange:
    def __init__(self, kind, arrs, rows=None, pad_rows=0, two_level=False):
        self.gather, self.n, self.pad = kind == "gather", len(arrs), pad_rows
        self.two_level = two_level and self.gather
        self.rows = [a.shape[0] if self.gather else rows for a in arrs]
        zeros = [jnp.zeros((pad_rows, a.shape[1]), a.dtype) for a in arrs] if pad_rows else []
        self.operands = list(arrs) + zeros
        if self.gather:
            self.out_shape = [jax.ShapeDtypeStruct((N_DEV * a.shape[0] + pad_rows, a.shape[1]), a.dtype) for a in arrs]
        else:
            self.out_shape = [jax.ShapeDtypeStruct((N_DEV, rows, a.shape[1]), a.dtype) for a in arrs]
        n_remote = self.n * len(_PEER_FLIPS)
        self.scratch = [pltpu.SemaphoreType.DMA((n_remote,)), pltpu.SemaphoreType.DMA((n_remote,)),
                        pltpu.SemaphoreType.DMA((2 * self.n,))]

    def _copies(self, ins, outs, sems):
        send, recv, loc = sems
        me, peers = _me_and_peers()

        def blk(ref, idx, r):
            return ref.at[pl.ds(pl.multiple_of(idx * r, 8), r)]

        local, sends, recvs = [], [], []
        for a in range(self.n):
            r = self.rows[a]
            if self.gather:
                local.append(pltpu.make_async_copy(ins[a], blk(outs[a], me, r), loc.at[2 * a]))
                if self.pad:
                    local.append(pltpu.make_async_copy(ins[self.n + a], outs[a].at[pl.ds(N_DEV * r, self.pad)],
                                                       loc.at[2 * a + 1]))
            else:
                local.append(pltpu.make_async_copy(blk(ins[a], me, r), outs[a].at[me], loc.at[2 * a]))
        for p, (dev, idx) in enumerate(peers):
            for a in range(self.n):
                r = self.rows[a]
                k = a * len(_PEER_FLIPS) + p
                if self.gather:
                    src, there, here = ins[a], blk(outs[a], me, r), blk(outs[a], idx, r)
                else:
                    src, there, here = blk(ins[a], idx, r), outs[a].at[me], outs[a].at[idx]
                sends.append(pltpu.make_async_remote_copy(src, there, send.at[k], recv.at[k], device_id=dev,
                                                          device_id_type=MESH))
                recvs.append(pltpu.make_async_remote_copy(src, here, send.at[k], recv.at[k], device_id=dev,
                                                          device_id_type=MESH))
        return local, sends, recvs

    def _copies_two_level(self, ins, outs, sems):
        send, recv, loc = sems
        x, y, c = (lax.axis_index(a) for a in AXES)
        me, sib = 4 * x + 2 * y + c, 4 * x + 2 * y + (1 - c)
        chips = [(1 - x, y), (x, 1 - y), (1 - x, 1 - y)]

        def blk(ref, idx, r):
            return ref.at[pl.ds(pl.multiple_of(idx * r, 8), r)]

        def remote(src, dst, k, dev):
            return pltpu.make_async_remote_copy(src, dst, send.at[k], recv.at[k], device_id=dev, device_id_type=MESH)

        local, first, passed = [], [], []
        for a in range(self.n):
            r = self.rows[a]
            local.append(pltpu.make_async_copy(ins[a], blk(outs[a], me, r), loc.at[2 * a]))
            if self.pad:
                local.append(pltpu.make_async_copy(ins[self.n + a], outs[a].at[pl.ds(N_DEV * r, self.pad)], loc.at[2 * a + 1]))
        k0 = len(_PEER_FLIPS)
        for a in range(self.n):
            r = self.rows[a]
            first.append((remote(ins[a], blk(outs[a], me, r), a * k0, (x, y, 1 - c)),
                          remote(ins[a], blk(outs[a], sib, r), a * k0, (x, y, 1 - c))))
        for j, (cx, cy) in enumerate(chips):
            for a in range(self.n):
                r = self.rows[a]
                theirs, sib_theirs = 4 * cx + 2 * cy + c, 4 * cx + 2 * cy + (1 - c)
                first.append((remote(ins[a], blk(outs[a], me, r), a * k0 + 1 + j, (cx, cy, c)), None))
                passed.append((remote(ins[a], blk(outs[a], theirs, r), a * k0 + 1 + j, (cx, cy, c)),
                               remote(blk(outs[a], theirs, r), blk(outs[a], theirs, r), a * k0 + 4 + j, (x, y, 1 - c)),
                               remote(blk(outs[a], theirs, r), blk(outs[a], sib_theirs, r), a * k0 + 4 + j, (x, y, 1 - c))))
        return local, first, passed

    def start(self, ins, outs, sems):
        if self.two_level:
            local, first, _ = self._copies_two_level(ins, outs, sems)
            for cp in local + [s for s, _ in first]:
                cp.start()
            return
        local, sends, _ = self._copies(ins, outs, sems)
        for cp in local + sends:
            cp.start()

    def middle(self, ins, outs, sems):
        if self.two_level:
            _, _, passed = self._copies_two_level(ins, outs, sems)
            for arrival, pass_on, _ in passed:
                arrival.wait_recv()
                pass_on.start()

    def wait(self, ins, outs, sems):
        if self.two_level:
            local, first, passed = self._copies_two_level(ins, outs, sems)
            for cp in local:
                cp.wait()
            for s, r in first:
                s.wait_send()
                if r is not None:
                    r.wait_recv()
            for _, pass_on, from_sibling in passed:
                pass_on.wait_send()
                from_sibling.wait_recv()
            return
        local, sends, recvs = self._copies(ins, outs, sems)
        for cp in local:
            cp.wait()
        for cp in sends:
            cp.wait_send()
        for cp in recvs:
            cp.wait_recv()


def _host_call(body, operands, *, grid, in_specs, out_specs, out_shape, scratch_shapes=(), semantics, name, carry=()):
    n_in, n_out, n_scr = len(in_specs), len(out_specs), len(scratch_shapes)

    def wrapped(*refs):
        pos = [0]

        def take(k):
            pos[0] += k
            return refs[pos[0] - k:pos[0]]

        ins, c_ins = take(n_in), [take(len(e.operands)) for e in carry]
        outs, c_outs = take(n_out), [take(e.n) for e in carry]
        scr, c_scr = take(n_scr), [take(len(e.scratch)) for e in carry]
        ids = [pl.program_id(a) for a in range(len(grid))]
        if carry:
            @pl.when(functools.reduce(jnp.logical_and, [i == 0 for i in ids]))
            def _():
                for e, ci, co, cs in zip(carry, c_ins, c_outs, c_scr):
                    e.start(ci, co, cs)

        if any(e.two_level for e in carry):
            @pl.when(functools.reduce(jnp.logical_and, [ids[0] == (3 * grid[0]) // 4] + [i == 0 for i in ids[1:]]))
            def _():
                for e, ci, co, cs in zip(carry, c_ins, c_outs, c_scr):
                    e.middle(ci, co, cs)

        body(*ins, *outs, *scr)
        if carry:
            @pl.when(functools.reduce(jnp.logical_and, [i == g - 1 for i, g in zip(ids, grid)]))
            def _():
                for e, ci, co, cs in zip(carry, c_ins, c_outs, c_scr):
                    e.wait(ci, co, cs)

    any_spec = pl.BlockSpec(memory_space=pl.ANY)
    res = pl.pallas_call(
        wrapped, grid=grid,
        in_specs=list(in_specs) + [any_spec] * sum(len(e.operands) for e in carry),
        out_specs=list(out_specs) + [any_spec] * sum(e.n for e in carry),
        out_shape=list(out_shape) + [s for e in carry for s in e.out_shape],
        scratch_shapes=list(scratch_shapes) + [s for e in carry for s in e.scratch],
        compiler_params=_cparams(*(["arbitrary"] * len(grid) if carry else semantics)), name=name,
    )(*operands, *[o for e in carry for o in e.operands])
    host, rest, carried = res[:n_out], res[n_out:], []
    for e in carry:
        carried.append(rest[:e.n])
        rest = rest[e.n:]
    return host, carried


def run_exchanges(exchanges, name):
    def body(*refs):
        pos = [0]

        def take(k):
            pos[0] += k
            return refs[pos[0] - k:pos[0]]

        c_ins = [take(len(e.operands)) for e in exchanges]
        c_outs = [take(e.n) for e in exchanges]
        c_scr = [take(len(e.scratch)) for e in exchanges]
        for e, ci, co, cs in zip(exchanges, c_ins, c_outs, c_scr):
            e.start(ci, co, cs)
        for e, ci, co, cs in zip(exchanges, c_ins, c_outs, c_scr):
            e.middle(ci, co, cs)
        for e, ci, co, cs in zip(exchanges, c_ins, c_outs, c_scr):
            e.wait(ci, co, cs)

    any_spec = pl.BlockSpec(memory_space=pl.ANY)
    res = pl.pallas_call(
        body,
        out_shape=[s for e in exchanges for s in e.out_shape],
        in_specs=[any_spec] * sum(len(e.operands) for e in exchanges),
        out_specs=[any_spec] * sum(e.n for e in exchanges),
        scratch_shapes=[s for e in exchanges for s in e.scratch],
        name=name,
    )(*[o for e in exchanges for o in e.operands])
    out = []
    for e in exchanges:
        out.append(res[:e.n])
        res = res[e.n:]
    return out


def sum_slots(land, name="sum_slots", transpose=False):
    _, rows, cols = land.shape
    tc = cols
    while tc % 256 == 0 and N_DEV * rows * tc * land.dtype.itemsize > SUM_BLOCK_BYTES:
        tc //= 2

    def body(l_ref, o_ref):
        acc = l_ref[0].astype(F32)
        for s in range(1, N_DEV):
            acc = acc + l_ref[s].astype(F32)
        o_ref[...] = acc.T if transpose else acc

    return pl.pallas_call(
        body, grid=(cols // tc,),
        in_specs=[pl.BlockSpec((N_DEV, rows, tc), lambda i: (0, 0, i))],
        out_specs=pl.BlockSpec((tc, rows), lambda i: (i, 0)) if transpose else pl.BlockSpec((rows, tc), lambda i: (0, i)),
        out_shape=jax.ShapeDtypeStruct((cols, rows) if transpose else (rows, cols), F32),
        compiler_params=_cparams("parallel"), name=name,
    )(land)


def cast_layer(w, l, transpose):
    _, rows, cols = w.shape

    def body(w_ref, o_ref):
        o_ref[...] = (w_ref[...].T if transpose else w_ref[...]).astype(BF16)

    out = (cols, rows) if transpose else (rows, cols)
    return pl.pallas_call(
        body, grid=(1,),
        in_specs=[pl.BlockSpec((None, rows, cols), lambda i: (l, 0, 0))],
        out_specs=pl.BlockSpec(out, lambda i: (0, 0)),
        out_shape=jax.ShapeDtypeStruct(out, BF16),
        compiler_params=_cparams("arbitrary"), name="cast_layer_t" if transpose else "cast_layer",
    )(w)


def ffn_fwd(x, gain, wg_t, wu_t, wd, carry=()):
    t, d = x.shape
    fp = wd.shape[0]
    tm = min(TM_FFN_FWD, t)
    n_t, n_f = t // tm, fp // TF

    def body(x_ref, g_ref, wg_ref, wu_ref, wd_ref, xo_ref, hn_ref, gate_ref, up_ref, act_ref):
        k = pl.program_id(1)

        @pl.when(k == 0)
        def _():
            for c in range(tm // ROW_CHUNK):
                rs = slice(c * ROW_CHUNK, (c + 1) * ROW_CHUNK)
                hn_ref[rs, :] = _rms_fwd(x_ref[rs, :], g_ref[...]).astype(BF16)
            xo_ref[...] = jnp.zeros_like(xo_ref)

        h = hn_ref[...]
        for c in range(TF // FF_CHUNK):
            cs = slice(c * FF_CHUNK, (c + 1) * FF_CHUNK)
            g = _nt(h, wg_ref[cs, :])
            u = _nt(h, wu_ref[cs, :])
            gate_ref[:, cs] = g.astype(BF16)
            up_ref[:, cs] = u.astype(BF16)
            act_ref[:, cs] = (g * jax.nn.sigmoid(g) * u).astype(BF16)
        xo_ref[...] += _nn(act_ref[...], wd_ref[...])

        @pl.when(k == n_f - 1)
        def _():
            xo_ref[...] = x_ref[...] + 0.5 * xo_ref[...]

    tok = pl.BlockSpec((tm, d), lambda i, k: (i, 0))
    tok_once = pl.BlockSpec((tm, d), lambda i, k: (i, 0), pipeline_mode=pl.Buffered(1))
    wsp = pl.BlockSpec((TF, d), lambda i, k: (k, 0))
    hid = pl.BlockSpec((tm, TF), lambda i, k: (i, k))
    return _host_call(
        body, (x, gain, wg_t, wu_t, wd), grid=(n_t, n_f),
        in_specs=[tok_once, pl.BlockSpec((1, d), lambda i, k: (0, 0)), wsp, wsp, wsp],
        out_specs=[tok, tok, hid, hid],
        out_shape=[jax.ShapeDtypeStruct((t, d), F32), jax.ShapeDtypeStruct((t, d), BF16),
                   jax.ShapeDtypeStruct((t, fp), BF16), jax.ShapeDtypeStruct((t, fp), BF16)],
        scratch_shapes=[pltpu.VMEM((tm, TF), BF16)],
        semantics=("parallel", "arbitrary"), name="ffn_fwd", carry=carry,
    )


def ffn_bwd_x(dxo, x, gain, gate, up, wg_t, wu_t, wd, carry=()):
    t, d = x.shape
    fp = wd.shape[0]
    n_t, n_f = t // TM, fp // TF

    def body(dxo_ref, x_ref, g_ref, gate_ref, up_ref, wg_ref, wu_ref, wd_ref,
             dxi_ref, dgate_ref, dup_ref, dy_ref, dgain_ref):
        i, k = pl.program_id(0), pl.program_id(1)

        @pl.when(k == 0)
        def _():
            dy_ref[...] = (0.5 * dxo_ref[...]).astype(BF16)
            dxi_ref[...] = jnp.zeros_like(dxi_ref)

        @pl.when(jnp.logical_and(i == 0, k == 0))
        def _():
            dgain_ref[...] = jnp.zeros_like(dgain_ref)

        dy = dy_ref[...]
        for c in range(TF // FF_CHUNK):
            cs = slice(c * FF_CHUNK, (c + 1) * FF_CHUNK)
            dact = _nt(dy, wd_ref[cs, :])
            silu, dsilu = _silu_parts(gate_ref[:, cs].astype(F32))
            dgate_ref[:, cs] = (dact * up_ref[:, cs].astype(F32) * dsilu).astype(BF16)
            dup_ref[:, cs] = (dact * silu).astype(BF16)
        dxi_ref[...] += _nn(dgate_ref[...], wg_ref[...]) + _nn(dup_ref[...], wu_ref[...])

        @pl.when(k == n_f - 1)
        def _():
            for c in range(TM // ROW_CHUNK):
                rs = slice(c * ROW_CHUNK, (c + 1) * ROW_CHUNK)
                dx, dg = _rms_bwd(x_ref[rs, :], g_ref[...], dxi_ref[rs, :])
                dxi_ref[rs, :] = dxo_ref[rs, :] + dx
                dgain_ref[...] += dg

    tok = pl.BlockSpec((TM, d), lambda i, k: (i, 0))
    row = pl.BlockSpec((1, d), lambda i, k: (0, 0))
    wsp = pl.BlockSpec((TF, d), lambda i, k: (k, 0))
    hid = pl.BlockSpec((TM, TF), lambda i, k: (i, k))
    return _host_call(
        body, (dxo, x, gain, gate, up, wg_t, wu_t, wd), grid=(n_t, n_f),
        in_specs=[tok, tok, row, hid, hid, wsp, wsp, wsp],
        out_specs=[tok, hid, hid, tok, row],
        out_shape=[jax.ShapeDtypeStruct((t, d), F32), jax.ShapeDtypeStruct((t, fp), BF16),
                   jax.ShapeDtypeStruct((t, fp), BF16), jax.ShapeDtypeStruct((t, d), BF16),
                   jax.ShapeDtypeStruct((1, d), F32)],
        semantics=("arbitrary", "arbitrary"), name="ffn_bwd_x", carry=carry,
    )


def ffn_bwd_hidden(dxo, gate, up, wd):
    t, d = dxo.shape
    fp = wd.shape[0]
    n_t, n_f = t // TM, fp // TF

    def body(dxo_ref, gate_ref, up_ref, wd_ref, dgate_ref, dup_ref, dy_ref):
        @pl.when(pl.program_id(1) == 0)
        def _():
            dy_ref[...] = (0.5 * dxo_ref[...]).astype(BF16)

        dy = dy_ref[...]
        for c in range(TF // FF_CHUNK):
            cs = slice(c * FF_CHUNK, (c + 1) * FF_CHUNK)
            dact = _nt(dy, wd_ref[cs, :])
            silu, dsilu = _silu_parts(gate_ref[:, cs].astype(F32))
            dgate_ref[:, cs] = (dact * up_ref[:, cs].astype(F32) * dsilu).astype(BF16)
            dup_ref[:, cs] = (dact * silu).astype(BF16)

    tok = pl.BlockSpec((TM, d), lambda i, k: (i, 0))
    hid = pl.BlockSpec((TM, TF), lambda i, k: (i, k))
    return pl.pallas_call(
        body, grid=(n_t, n_f),
        in_specs=[tok, hid, hid, pl.BlockSpec((TF, d), lambda i, k: (k, 0))],
        out_specs=[hid, hid, tok],
        out_shape=[jax.ShapeDtypeStruct((t, fp), BF16), jax.ShapeDtypeStruct((t, fp), BF16),
                   jax.ShapeDtypeStruct((t, d), BF16)],
        compiler_params=_cparams("parallel", "arbitrary"), name="ffn_bwd_hidden",
    )(dxo, gate, up, wd)


def ffn_bwd_input(dxo, x, gain, dgate, dup, wg_t, wu_t, carry=()):
    t, d = x.shape
    fp = wg_t.shape[0]
    n_t, n_f = t // TM, fp // TF

    def body(dxo_ref, x_ref, g_ref, dgate_ref, dup_ref, wg_ref, wu_ref, dxi_ref, dgain_ref):
        i, k = pl.program_id(0), pl.program_id(1)

        @pl.when(k == 0)
        def _():
            dxi_ref[...] = jnp.zeros_like(dxi_ref)

        @pl.when(jnp.logical_and(i == 0, k == 0))
        def _():
            dgain_ref[...] = jnp.zeros_like(dgain_ref)

        dxi_ref[...] += _nn(dgate_ref[...], wg_ref[...]) + _nn(dup_ref[...], wu_ref[...])

        @pl.when(k == n_f - 1)
        def _():
            for c in range(TM // ROW_CHUNK):
                rs = slice(c * ROW_CHUNK, (c + 1) * ROW_CHUNK)
                dx, dg = _rms_bwd(x_ref[rs, :], g_ref[...], dxi_ref[rs, :])
                dxi_ref[rs, :] = dxo_ref[rs, :] + dx
                dgain_ref[...] += dg

    tok = pl.BlockSpec((TM, d), lambda i, k: (i, 0))
    row = pl.BlockSpec((1, d), lambda i, k: (0, 0))
    wsp = pl.BlockSpec((TF, d), lambda i, k: (k, 0))
    hid = pl.BlockSpec((TM, TF), lambda i, k: (i, k))
    return _host_call(
        body, (dxo, x, gain, dgate, dup, wg_t, wu_t), grid=(n_t, n_f),
        in_specs=[tok, tok, row, hid, hid, wsp, wsp],
        out_specs=[tok, row],
        out_shape=[jax.ShapeDtypeStruct((t, d), F32), jax.ShapeDtypeStruct((1, d), F32)],
        semantics=("arbitrary", "arbitrary"), name="ffn_bwd_input", carry=carry,
    )


def ffn_bwd_w(hn, dy, gate, up, dgate, dup, carry=()):
    t, d = hn.shape
    fp = gate.shape[1]
    tm = min(TM_FFN_BWD_W, t)
    n_t, n_f = t // tm, fp // TF

    def body(hn_ref, dy_ref, gate_ref, up_ref, dgate_ref, dup_ref, dwg_ref, dwu_ref, dwd_ref, ag_ref, au_ref, ad_ref):
        i = pl.program_id(1)

        @pl.when(i == 0)
        def _():
            ag_ref[...] = jnp.zeros_like(ag_ref)
            au_ref[...] = jnp.zeros_like(au_ref)
            ad_ref[...] = jnp.zeros_like(ad_ref)

        h = hn_ref[...]
        ag_ref[...] += _tn(dgate_ref[...], h)
        au_ref[...] += _tn(dup_ref[...], h)
        silu, _ = _silu_parts(gate_ref[...].astype(F32))
        act = (silu * up_ref[...].astype(F32)).astype(BF16)
        ad_ref[...] += _tn(act, dy_ref[...])

        @pl.when(i == n_t - 1)
        def _():
            dwg_ref[...] = ag_ref[...].astype(BF16)
            dwu_ref[...] = au_ref[...].astype(BF16)
            dwd_ref[...] = ad_ref[...].astype(BF16)

    tok = pl.BlockSpec((tm, d), lambda k, i: (i, 0))
    hid = pl.BlockSpec((tm, TF), lambda k, i: (i, k))
    wsp = pl.BlockSpec((TF, d), lambda k, i: (k, 0))
    return _host_call(
        body, (hn, dy, gate, up, dgate, dup), grid=(n_f, n_t),
        in_specs=[tok, tok, hid, hid, hid, hid],
        out_specs=[wsp, wsp, wsp],
        out_shape=[jax.ShapeDtypeStruct((fp, d), BF16)] * 3,
        scratch_shapes=[pltpu.VMEM((TF, d), F32)] * 3,
        semantics=("parallel", "arbitrary"), name="ffn_bwd_w", carry=carry,
    )


def rms_fwd(x, gain):
    t, d = x.shape

    def body(x_ref, g_ref, o_ref):
        o_ref[...] = _rms_fwd(x_ref[...], g_ref[...]).astype(BF16)

    tok = pl.BlockSpec((TM, d), lambda i: (i, 0))
    return pl.pallas_call(
        body, grid=(t // TM,), in_specs=[tok, pl.BlockSpec((1, d), lambda i: (0, 0))], out_specs=tok,
        out_shape=jax.ShapeDtypeStruct((t, d), BF16), compiler_params=_cparams("parallel"), name="rms_fwd",
    )(x, gain)


def rms_bwd(x, gain, dh, dx_in):
    t, d = x.shape

    def body(x_ref, g_ref, dh_ref, dxi_ref, dxo_ref, dg_ref):
        i = pl.program_id(0)

        @pl.when(i == 0)
        def _():
            dg_ref[...] = jnp.zeros_like(dg_ref)

        dx, dg = _rms_bwd(x_ref[...], g_ref[...], dh_ref[...])
        dxo_ref[...] = dxi_ref[...] + dx
        dg_ref[...] += dg

    tok = pl.BlockSpec((TM, d), lambda i: (i, 0))
    row = pl.BlockSpec((1, d), lambda i: (0, 0))
    return pl.pallas_call(
        body, grid=(t // TM,), in_specs=[tok, row, tok, tok], out_specs=[tok, row],
        out_shape=[jax.ShapeDtypeStruct((t, d), F32), jax.ShapeDtypeStruct((1, d), F32)],
        compiler_params=_cparams("arbitrary"), name="rms_bwd",
    )(x, gain, dh, dx_in)


def proj_nn(a, w, *, head_gain=None, residual=None, out_dtype=F32, name="proj_nn"):
    t, kd = a.shape
    n = w.shape[1]
    tn = min(PROJ_CHUNK, n)

    def body(*refs):
        a_ref, w_ref = refs[:2]
        for c in range(n // tn):
            cs = slice(c * tn, (c + 1) * tn)
            acc = _nn(a_ref[...], w_ref[:, cs])
            if head_gain is not None:
                g_ref, o_ref, on_ref = refs[2:]
                o_ref[:, cs] = acc
                for j in range(tn // HEAD_DIM):
                    sl = slice(j * HEAD_DIM, (j + 1) * HEAD_DIM)
                    on_ref[:, c * tn + j * HEAD_DIM:c * tn + (j + 1) * HEAD_DIM] = _rms_fwd(acc[:, sl], g_ref[...]).astype(BF16)
            elif residual is not None:
                r_ref, o_ref = refs[2:]
                o_ref[:, cs] = r_ref[:, cs] + acc
            else:
                refs[2][:, cs] = acc.astype(out_dtype)

    a_spec = pl.BlockSpec((TM, kd), lambda i: (i, 0))
    w_spec = pl.BlockSpec((kd, n), lambda i: (0, 0), pipeline_mode=pl.Buffered(1))
    o_spec = pl.BlockSpec((TM, n), lambda i: (i, 0))
    ins, in_specs = [a, w], [a_spec, w_spec]
    if head_gain is not None:
        ins.append(head_gain)
        in_specs.append(pl.BlockSpec((1, HEAD_DIM), lambda i: (0, 0)))
        out_specs = [o_spec, o_spec]
        out_shape = [jax.ShapeDtypeStruct((t, n), F32), jax.ShapeDtypeStruct((t, n), BF16)]
    elif residual is not None:
        ins.append(residual)
        in_specs.append(o_spec)
        out_specs, out_shape = o_spec, jax.ShapeDtypeStruct((t, n), F32)
    else:
        out_specs, out_shape = o_spec, jax.ShapeDtypeStruct((t, n), out_dtype)
    return pl.pallas_call(
        body, grid=(t // TM,), in_specs=in_specs, out_specs=out_specs, out_shape=out_shape,
        compiler_params=_cparams("parallel"), name=name,
    )(*ins)


def proj_nt(a, w, *, addend=None, out_dtype=F32, name="proj_nt"):
    t, n = a.shape
    kd = w.shape[0]
    tk = min(PROJ_CHUNK, kd)

    def body(*refs):
        a_ref, w_ref = refs[:2]
        for c in range(kd // tk):
            cs = slice(c * tk, (c + 1) * tk)
            acc = _nt(a_ref[...], w_ref[cs, :])
            if addend is not None:
                acc = acc + refs[2][:, cs]
            refs[-1][:, cs] = acc.astype(out_dtype)

    o_spec = pl.BlockSpec((TM, kd), lambda i: (i, 0))
    ins = [a, w] + ([addend] if addend is not None else [])
    in_specs = [pl.BlockSpec((TM, n), lambda i: (i, 0)),
                pl.BlockSpec((kd, n), lambda i: (0, 0), pipeline_mode=pl.Buffered(1))]
    if addend is not None:
        in_specs.append(o_spec)
    return pl.pallas_call(
        body, grid=(t // TM,), in_specs=in_specs, out_specs=o_spec,
        out_shape=jax.ShapeDtypeStruct((t, kd), out_dtype),
        compiler_params=_cparams("parallel"), name=name,
    )(*ins)


def grad_tn(a, b, name="grad_tn"):
    t, m = a.shape
    n = b.shape[1]
    tmo, tno = min(1024, m), min(1024, n)
    tk = min(TK_GRAD, t)
    n_t = t // tk

    def body(a_ref, b_ref, o_ref, acc_ref):
        k = pl.program_id(2)

        @pl.when(k == 0)
        def _():
            acc_ref[...] = jnp.zeros_like(acc_ref)

        acc_ref[...] += _tn(a_ref[...], b_ref[...])

        @pl.when(k == n_t - 1)
        def _():
            o_ref[...] = acc_ref[...].astype(BF16)

    return pl.pallas_call(
        body, grid=(m // tmo, n // tno, n_t),
        in_specs=[pl.BlockSpec((tk, tmo), lambda i, j, k: (k, i)), pl.BlockSpec((tk, tno), lambda i, j, k: (k, j))],
        out_specs=pl.BlockSpec((tmo, tno), lambda i, j, k: (i, j)),
        out_shape=jax.ShapeDtypeStruct((m, n), BF16),
        scratch_shapes=[pltpu.VMEM((tmo, tno), F32)],
        compiler_params=_cparams("parallel", "parallel", "arbitrary"), name=name,
    )(a, b)


def head_rms_bwd(dy, pre, gain):
    t, d = pre.shape

    def body(dy_ref, p_ref, g_ref, o_ref, dg_ref):
        i = pl.program_id(0)

        @pl.when(i == 0)
        def _():
            dg_ref[...] = jnp.zeros_like(dg_ref)

        dg_acc = jnp.zeros((1, HEAD_DIM), F32)
        for j in range(d // HEAD_DIM):
            sl = slice(j * HEAD_DIM, (j + 1) * HEAD_DIM)
            dx, dg = _rms_bwd(p_ref[:, sl], g_ref[...], dy_ref[:, sl])
            o_ref[:, sl] = dx.astype(BF16)
            dg_acc = dg_acc + dg
        dg_ref[...] += dg_acc

    tok = pl.BlockSpec((TM, d), lambda i: (i, 0))
    row = pl.BlockSpec((1, HEAD_DIM), lambda i: (0, 0))
    return pl.pallas_call(
        body, grid=(t // TM,), in_specs=[tok, tok, row], out_specs=[tok, row],
        out_shape=[jax.ShapeDtypeStruct((t, d), BF16), jax.ShapeDtypeStruct((1, HEAD_DIM), F32)],
        compiler_params=_cparams("arbitrary"), name="head_rms_bwd",
    )(dy, pre, gain)


def band_bias(table):
    h = table.shape[0]
    far = jnp.broadcast_to(table[:, N_REL - 1:], (h, LEFT - CHUNK))
    w = jnp.concatenate([far, jnp.flip(table[:, :N_REL - 1], axis=1)], axis=1)
    band = jnp.stack([w[:, CHUNK - 1 - r:CHUNK - 1 - r + LEFT + CHUNK] for r in range(CHUNK)], axis=1)
    out = jnp.full((h, BQ, BQ + LEFT), NEG_INF, F32)
    for a in range(BQ // CHUNK):
        out = out.at[:, a * CHUNK:(a + 1) * CHUNK, a * CHUNK:a * CHUNK + LEFT + CHUNK].set(band)
    return out


def _attn_probs(q, k, b_ref, start):
    w = k.shape[0]
    s = _nt(q, k) * (HEAD_DIM ** -0.5) + b_ref[...]
    kpos = start + lax.broadcasted_iota(jnp.int32, (1, w), 1)
    s = jnp.where(kpos >= LEFT, s, NEG_INF)
    e = jnp.exp(s - jnp.max(s, axis=-1, keepdims=True))
    return e / jnp.sum(e, axis=-1, keepdims=True)


def attn_fwd(qn, k_pad, v_pad, bias):
    t, d = qn.shape
    tp = k_pad.shape[0]
    w = BQ + LEFT

    def body(q_ref, k_ref, v_ref, b_ref, o_ref):
        i = pl.program_id(1)
        for sub in range(QB_STEP):
            rs = slice(sub * BQ, (sub + 1) * BQ)
            start = pl.multiple_of((i * QB_STEP + sub) * BQ, BQ)
            p = _attn_probs(q_ref[rs, :], k_ref[pl.ds(start, w), :], b_ref, start)
            o_ref[rs, :] = _nn(p.astype(BF16), v_ref[pl.ds(start, w), :]).astype(BF16)

    q_spec = pl.BlockSpec((QB_STEP * BQ, HEAD_DIM), lambda h, i: (i, h))
    kv_spec = pl.BlockSpec((tp, HEAD_DIM), lambda h, i: (0, h))
    return pl.pallas_call(
        body, grid=(d // HEAD_DIM, t // (QB_STEP * BQ)),
        in_specs=[q_spec, kv_spec, kv_spec, pl.BlockSpec((None, BQ, w), lambda h, i: (h, 0, 0))],
        out_specs=q_spec, out_shape=jax.ShapeDtypeStruct((t, d), BF16),
        compiler_params=_cparams("parallel", "parallel"), name="attn_fwd",
    )(qn, k_pad, v_pad, bias)


def attn_bwd(qn, k_pad, v_pad, bias, do):
    t, d = qn.shape
    tp = k_pad.shape[0]
    w = BQ + LEFT
    scale = HEAD_DIM ** -0.5

    def body(q_ref, k_ref, v_ref, b_ref, do_ref, dq_ref, dk_ref, dv_ref, db_ref):
        i = pl.program_id(1)

        @pl.when(i == 0)
        def _():
            dk_ref[...] = jnp.zeros_like(dk_ref)
            dv_ref[...] = jnp.zeros_like(dv_ref)
            db_ref[...] = jnp.zeros_like(db_ref)

        for sub in range(QB_STEP):
            rs = slice(sub * BQ, (sub + 1) * BQ)
            start = pl.multiple_of((i * QB_STEP + sub) * BQ, BQ)
            q, do = q_ref[rs, :], do_ref[rs, :]
            k = k_ref[pl.ds(start, w), :]
            v = v_ref[pl.ds(start, w), :]
            p = _attn_probs(q, k, b_ref, start)
            dp = _nt(do, v)
            ds = p * (dp - jnp.sum(p * dp, axis=-1, keepdims=True))
            db_ref[...] += ds
            dsb = ds.astype(BF16)
            dq_ref[rs, :] = _nn(dsb, k) * scale
            dk_ref[pl.ds(start, w), :] += _tn(dsb, q) * scale
            dv_ref[pl.ds(start, w), :] += _tn(p.astype(BF16), do)

    q_spec = pl.BlockSpec((QB_STEP * BQ, HEAD_DIM), lambda h, i: (i, h))
    kv_spec = pl.BlockSpec((tp, HEAD_DIM), lambda h, i: (0, h))
    b_spec = pl.BlockSpec((None, BQ, w), lambda h, i: (h, 0, 0))
    return pl.pallas_call(
        body, grid=(d // HEAD_DIM, t // (QB_STEP * BQ)),
        in_specs=[q_spec, kv_spec, kv_spec, b_spec, q_spec],
        out_specs=[q_spec, kv_spec, kv_spec, b_spec],
        out_shape=[jax.ShapeDtypeStruct((t, d), F32), jax.ShapeDtypeStruct((tp, d), F32),
                   jax.ShapeDtypeStruct((tp, d), F32), jax.ShapeDtypeStruct(bias.shape, F32)],
        compiler_params=_cparams("parallel", "arbitrary"), name="attn_bwd",
    )(qn, k_pad, v_pad, bias, do)


def _pool_diff(ext_ref, h, t0, g, cg):
    tm = h.shape[0]
    cs = slice(g * cg, (g + 1) * cg)
    win = POOL_WINDOWS[g]
    s = ext_ref[HALO:HALO + tm, cs]
    for j in range(1, win):
        s = s + ext_ref[HALO - j:HALO - j + tm, cs]
    cnt = jnp.minimum(t0 + lax.broadcasted_iota(jnp.int32, (tm, 1), 0) + 1, win).astype(F32)
    return (s / cnt - h[:, cs]).astype(BF16), cnt


def pool_fwd(x, gain, w, scale):
    t, d = x.shape
    n_g = len(POOL_WINDOWS)
    cg = d // n_g

    def body(x_ref, g_ref, w_ref, s_ref, o_ref, ext_ref):
        i = pl.program_id(0)
        h = _rms_fwd(x_ref[...], g_ref[...])

        @pl.when(i == 0)
        def _():
            ext_ref[0:HALO, :] = jnp.zeros((HALO, d), F32)

        @pl.when(i > 0)
        def _():
            ext_ref[0:HALO, :] = ext_ref[TM:TM + HALO, :]

        ext_ref[HALO:HALO + TM, :] = h
        for g in range(n_g):
            cs = slice(g * cg, (g + 1) * cg)
            diff, _ = _pool_diff(ext_ref, h, i * TM, g, cg)
            o_ref[:, cs] = x_ref[:, cs] + _nn(diff, w_ref[g]) * s_ref[:, cs]

    tok = pl.BlockSpec((TM, d), lambda i: (i, 0))
    row = pl.BlockSpec((1, d), lambda i: (0, 0))
    return pl.pallas_call(
        body, grid=(t // TM,),
        in_specs=[tok, row, pl.BlockSpec((n_g, cg, cg), lambda i: (0, 0, 0)), row],
        out_specs=tok, out_shape=jax.ShapeDtypeStruct((t, d), F32),
        scratch_shapes=[pltpu.VMEM((HALO + TM, d), F32)],
        compiler_params=_cparams("arbitrary"), name="pool_fwd",
    )(x, gain, w, scale)


def pool_bwd(x, dxo, gain, w, scale):
    t, d = x.shape
    n_g = len(POOL_WINDOWS)
    cg = d // n_g
    n_t = t // TM

    def body(x_ref, xh_ref, dxo_ref, g_ref, w_ref, s_ref, dxi_ref, dw_ref, ds_ref, dg_ref, exth_ref, extq_ref, dh_ref):
        i = pl.program_id(0)
        ti = n_t - 1 - i

        @pl.when(i == 0)
        def _():
            dw_ref[...] = jnp.zeros_like(dw_ref)
            ds_ref[...] = jnp.zeros_like(ds_ref)
            dg_ref[...] = jnp.zeros_like(dg_ref)
            extq_ref[TM:TM + HALO, :] = jnp.zeros((HALO, d), F32)

        @pl.when(i > 0)
        def _():
            extq_ref[TM:TM + HALO, :] = extq_ref[0:HALO, :]

        xv = x_ref[...]
        h = _rms_fwd(xv, g_ref[...])
        exth_ref[0:HALO, :] = jnp.where(ti > 0, _rms_fwd(xh_ref[...], g_ref[...]), 0.0)
        exth_ref[HALO:HALO + TM, :] = h
        for g in range(n_g):
            cs = slice(g * cg, (g + 1) * cg)
            diff, cnt = _pool_diff(exth_ref, h, ti * TM, g, cg)
            dxo_g = dxo_ref[:, cs]
            ds_ref[:, cs] += jnp.sum(dxo_g * _nn(diff, w_ref[g]), axis=0, keepdims=True)
            dy = (dxo_g * s_ref[:, cs]).astype(BF16)
            dw_ref[g] += _tn(diff, dy)
            ddiff = _nt(dy, w_ref[g])
            extq_ref[0:TM, cs] = ddiff / cnt
            acc = -ddiff
            for j in range(POOL_WINDOWS[g]):
                acc = acc + extq_ref[j:j + TM, cs]
            dh_ref[:, cs] = acc
        dx, dg = _rms_bwd(xv, g_ref[...], dh_ref[...])
        dxi_ref[...] = dxo_ref[...] + dx
        dg_ref[...] += dg

    tok = pl.BlockSpec((TM, d), lambda i: (n_t - 1 - i, 0))
    halo = pl.BlockSpec((HALO, d), lambda i: (jnp.maximum((n_t - 1 - i) * (TM // HALO) - 1, 0), 0))
    row = pl.BlockSpec((1, d), lambda i: (0, 0))
    wsp = pl.BlockSpec((n_g, cg, cg), lambda i: (0, 0, 0))
    return pl.pallas_call(
        body, grid=(n_t,),
        in_specs=[tok, halo, tok, row, wsp, row],
        out_specs=[tok, wsp, row, row],
        out_shape=[jax.ShapeDtypeStruct((t, d), F32), jax.ShapeDtypeStruct((n_g, cg, cg), F32),
                   jax.ShapeDtypeStruct((1, d), F32), jax.ShapeDtypeStruct((1, d), F32)],
        scratch_shapes=[pltpu.VMEM((HALO + TM, d), F32), pltpu.VMEM((TM + HALO, d), F32), pltpu.VMEM((TM, d), F32)],
        compiler_params=_cparams("arbitrary"), name="pool_bwd",
    )(x, x, dxo, gain, w, scale)


def loss_head(y, target):
    t, d = y.shape

    def body(y_ref, t_ref, sq_ref, dy_ref):
        i = pl.program_id(0)

        @pl.when(i == 0)
        def _():
            sq_ref[...] = jnp.zeros_like(sq_ref)

        err = y_ref[...] - t_ref[...]
        sq_ref[...] += jnp.sum(err * err, axis=0, keepdims=True)
        dy_ref[...] = err * (1.0 / d)

    tok = pl.BlockSpec((TM, d), lambda i: (i, 0))
    row = pl.BlockSpec((1, d), lambda i: (0, 0))
    return pl.pallas_call(
        body, grid=(t // TM,), in_specs=[tok, tok], out_specs=[row, tok],
        out_shape=[jax.ShapeDtypeStruct((1, d), F32), jax.ShapeDtypeStruct((t, d), F32)],
        compiler_params=_cparams("arbitrary"), name="loss_head",
    )(y, target)


def adamw(w, m, v, g, name="adamw"):
    rows, cols = w.shape
    tr = _row_tile(rows)

    def body(w_ref, m_ref, v_ref, g_ref, d_ref, mo_ref, vo_ref):
        gv = g_ref[...]
        mn = ADAM_B1 * m_ref[...] + (1.0 - ADAM_B1) * gv
        vn = ADAM_B2 * v_ref[...] + (1.0 - ADAM_B2) * jnp.square(gv)
        m_hat = mn / (1.0 - ADAM_B1 ** ADAM_STEP)
        v_hat = vn / (1.0 - ADAM_B2 ** ADAM_STEP)
        d_ref[...] = -ADAM_LR * (m_hat / (jnp.sqrt(v_hat) + ADAM_EPS) + ADAM_WD * w_ref[...])
        mo_ref[...] = mn
        vo_ref[...] = vn

    blk = pl.BlockSpec((tr, cols), lambda i: (i, 0))
    return pl.pallas_call(
        body, grid=(rows // tr,), in_specs=[blk] * 4, out_specs=[blk] * 3,
        out_shape=[jax.ShapeDtypeStruct((rows, cols), F32)] * 3,
        compiler_params=_cparams("parallel"), name=name,
    )(w, m, v, g)


def _adamw_nd(w, m, v, g):
    shape = w.shape
    cols = shape[-1]
    d, mn, vn = adamw(w.reshape(-1, cols), m.reshape(-1, cols), v.reshape(-1, cols), g.reshape(-1, cols))
    return d.reshape(shape), mn.reshape(shape), vn.reshape(shape)


def _pack_small(d, parts):
    flat = jnp.concatenate([p.reshape(-1).astype(F32) for p in parts])
    rows = -(-flat.shape[0] // d)
    rows = -(-rows // 8) * 8
    return jnp.pad(flat, (0, rows * d - flat.shape[0])).reshape(rows, d)


def _unpack_small(packed, shapes):
    flat = packed.reshape(-1)
    out, off = [], 0
    for s in shapes:
        n = 1
        for k in s:
            n *= k
        out.append(flat[off:off + n].reshape(s))
        off += n
    return out


def kernel(x, ffn1_norm, ffn1_w_gate, ffn1_w_up, ffn1_w_down, mix_norm, ffn2_norm, ffn2_w_gate, ffn2_w_up, ffn2_w_down, pool_w, pool_scale, kv_norm, w_k, w_v, k_gain, w_q, q_gain, rel_bias, w_o, loss_target, m_ffn1_norm, m_ffn1_w_gate, m_ffn1_w_up, m_ffn1_w_down, m_mix_norm, m_ffn2_norm, m_ffn2_w_gate, m_ffn2_w_up, m_ffn2_w_down, m_pool_w, m_pool_scale, m_kv_norm, m_w_k, m_w_v, m_k_gain, m_w_q, m_q_gain, m_rel_bias, m_w_o, v_ffn1_norm, v_ffn1_w_gate, v_ffn1_w_up, v_ffn1_w_down, v_mix_norm, v_ffn2_norm, v_ffn2_w_gate, v_ffn2_w_up, v_ffn2_w_down, v_pool_w, v_pool_scale, v_kv_norm, v_w_k, v_w_v, v_k_gain, v_w_q, v_q_gain, v_rel_bias, v_w_o):
    x0, target = x[0], loss_target[0]
    t, d = x0.shape
    depth = ffn1_norm.shape[0]
    n_b = depth - N_A_LAYERS
    fs = ffn1_w_gate.shape[-1]
    f_pad = -(-(N_DEV * fs) // TF) * TF - N_DEV * fs
    n_g = len(POOL_WINDOWS)
    cg = d // n_g
    ds_ = d // N_DEV
    me = 4 * lax.axis_index("x") + 2 * lax.axis_index("y") + lax.axis_index("c")

    ffn_in = ((ffn1_norm, ffn1_w_gate, ffn1_w_up, ffn1_w_down), (ffn2_norm, ffn2_w_gate, ffn2_w_up, ffn2_w_down))

    def ffn_gather(l, s):
        _, wg, wu, wd = ffn_in[s]
        rows = [cast_layer(wg, l, True), cast_layer(wu, l, True), cast_layer(wd, l, False)]
        return _Exchange("gather", rows, pad_rows=f_pad, two_level=True)

    scale_rows = jnp.pad(pool_scale, ((0, 8 - pool_scale.shape[0]), (0, 0)))
    mix_groups = [[pool_w[l].reshape(n_g * cg // N_DEV, cg).astype(BF16) for l in range(N_A_LAYERS)] + [scale_rows],
                  [w_k.astype(BF16), w_v.astype(BF16)]]
    mix_groups += [[w_q[b].astype(BF16), w_o[b].astype(BF16)] for b in range(n_b)]
    mix_got = {}
    bias = [band_bias(rel_bias[b]) for b in range(n_b)]

    def ffn_forward(xin, l, s):
        n = 2 * l + s
        carry = [ffn_gather((n + 1) // 2, (n + 1) % 2)] if n + 1 < 2 * depth else []
        if n < len(mix_groups):
            carry.append(_Exchange("gather", mix_groups[n], two_level=True))
        res, got = ffn_fwd(xin, ffn_in[s][0][l:l + 1], *ffn_w[s, l], carry=carry)
        if n + 1 < 2 * depth:
            ffn_w[(n + 1) % 2, (n + 1) // 2] = got[0]
        if n < len(mix_groups):
            mix_got[n] = got[-1]
        return res

    ffn_w = {(0, 0): run_exchanges([ffn_gather(0, 0)], "gather_first")[0]}
    saved = []
    xc = x0
    k_pad = v_pad = kv_saved = None
    for l in range(depth):
        rec = {"x_a": xc}
        xb, rec["hn1"], rec["g1"], rec["u1"] = ffn_forward(xc, l, 0)
        if l == 0:
            pool_f = [p.reshape(N_DEV, n_g, cg // N_DEV, cg).transpose(1, 0, 2, 3).reshape(n_g, cg, cg)
                      for p in mix_got[0][:N_A_LAYERS]]
            scale_f = mix_got[0][-1].reshape(N_DEV, 8, ds_)[:, :N_A_LAYERS].transpose(1, 0, 2).reshape(N_A_LAYERS, d)
        rec["x_b"] = xb
        if l < N_A_LAYERS:
            xm = pool_fwd(xb, mix_norm[l:l + 1], pool_f[l], scale_f[l:l + 1])
        else:
            b = l - N_A_LAYERS
            hm = rms_fwd(xb, mix_norm[l:l + 1])
            q_pre, qn = proj_nn(hm, wq_f[b], head_gain=q_gain[b:b + 1], name="proj_q")
            o = attn_fwd(qn, k_pad, v_pad, bias[b])
            xm = proj_nn(o, wo_f[b], residual=xb, name="proj_o")
            rec.update(hm=hm, q_pre=q_pre, qn=qn, o=o)
        rec["x_c"] = xm
        xc, rec["hn2"], rec["g2"], rec["u2"] = ffn_forward(xm, l, 1)
        if l == N_A_LAYERS - 1:
            wk_f, wv_f = mix_got[1]
            wq_f, wo_f = [mix_got[2 + b][0] for b in range(n_b)], [mix_got[2 + b][1] for b in range(n_b)]
            hk = rms_fwd(xc, kv_norm[None])
            k_pre, kn = proj_nn(hk, wk_f, head_gain=k_gain[None], name="proj_q")
            vv = proj_nn(hk, wv_f, out_dtype=BF16, name="proj_v")
            k_pad = jnp.pad(kn, ((LEFT, 0), (0, 0)))
            v_pad = jnp.pad(vv, ((LEFT, 0), (0, 0)))
            kv_saved = dict(x=xc, hk=hk, k_pre=k_pre)
        saved.append(rec)
    sq, dx = loss_head(xc, target)
    loss = lax.psum(0.5 * jnp.sum(sq) / d, AXES)

    g_ffn = {}
    g_norm = {0: [None] * depth, 1: [None] * depth, 2: [None] * depth}
    g_wq, g_wo, g_pool, g_qgain, g_bias, g_pscale = [None] * n_b, [None] * n_b, [None] * N_A_LAYERS, [None] * n_b, [None] * n_b, [None] * N_A_LAYERS
    dk_sum = dv_sum = None

    pending = []
    outbox = []
    mix_land = {}

    def ffn_sums(land):
        return [sum_slots(z, "sum_ffn_t" if j < 2 else "sum_ffn", transpose=j < 2) for j, z in enumerate(land)]

    def take_ffn(got):
        for (key, _), land in zip(pending, got):
            g_ffn[key] = ffn_sums(land)

    def ffn_backward(dx, l, s, x_in, hn, gate, up):
        gain = ffn_in[s][0][l:l + 1]
        ffn_carry = [_Exchange("scatter", dws, rows=fs) for _, dws in pending]
        box = list(outbox)
        outbox.clear()
        if (l, s) != (0, 0):
            (dx, dgate, dup, dy, dgain), got = ffn_bwd_x(dx, x_in, gain, gate, up, *ffn_w[s, l], carry=ffn_carry)
            take_ffn(got)
            dws, got = ffn_bwd_w(hn, dy, gate, up, dgate, dup, carry=[e for e, _ in box])
            pending[:] = [((s, l), dws)]
        else:
            wg_t, wu_t, wd_f = ffn_w[s, l]
            dgate, dup, dy = ffn_bwd_hidden(dx, gate, up, wd_f)
            dws, got = ffn_bwd_w(hn, dy, gate, up, dgate, dup, carry=ffn_carry + [e for e, _ in box])
            take_ffn(got)
            got = got[len(ffn_carry):]
            (dx, dgain), last = ffn_bwd_input(dx, x_in, gain, dgate, dup, wg_t, wu_t, carry=[_Exchange("scatter", dws, rows=fs)])
            g_ffn[s, l] = ffn_sums(last[0])
            pending.clear()
        for (_, take), res in zip(box, got):
            take(res)
        g_norm[2 * s][l] = dgain
        return dx

    def pool_grad_rows(g):
        return g.reshape(n_g, N_DEV, cg // N_DEV, cg).transpose(1, 0, 2, 3).reshape(n_g * cg, cg).astype(BF16)

    for l in reversed(range(depth)):
        rec = saved[l]
        dx = ffn_backward(dx, l, 1, rec["x_c"], rec["hn2"], rec["g2"], rec["u2"])
        if l < N_A_LAYERS:
            dx, g_pool, g_pscale[l], g_norm[1][l] = pool_bwd(rec["x_b"], dx, mix_norm[l:l + 1], pool_f[l], scale_f[l:l + 1])
            outbox.append((_Exchange("scatter", [pool_grad_rows(g_pool)], rows=ds_),
                           functools.partial(mix_land.__setitem__, ("pool", l))))
        else:
            b = l - N_A_LAYERS
            dxb = dx.astype(BF16)
            do = proj_nt(dxb, wo_f[b], out_dtype=BF16, name="proj_do")
            g_wo[b] = grad_tn(rec["o"], dxb)
            dq, dk, dv, dbias = attn_bwd(rec["qn"], k_pad, v_pad, bias[b], do)
            dk_sum = dk if dk_sum is None else dk_sum + dk
            dv_sum = dv if dv_sum is None else dv_sum + dv
            g_bias[b] = jax.vjp(band_bias, rel_bias[b])[1](dbias)[0]
            dq_pre, g_qgain[b] = head_rms_bwd(dq, rec["q_pre"], q_gain[b:b + 1])
            g_wq[b] = grad_tn(rec["hm"], dq_pre)
            dhm = proj_nt(dq_pre, wq_f[b], name="proj_dh")
            dx, g_norm[1][l] = rms_bwd(rec["x_b"], mix_norm[l:l + 1], dhm, dx)
        dx = ffn_backward(dx, l, 0, rec["x_a"], rec["hn1"], rec["g1"], rec["u1"])
        if l == N_A_LAYERS:
            dk_pre, g_kgain = head_rms_bwd(dk_sum[LEFT:], kv_saved["k_pre"], k_gain[None])
            dvb = dv_sum[LEFT:].astype(BF16)
            g_wk = grad_tn(kv_saved["hk"], dk_pre)
            g_wv = grad_tn(kv_saved["hk"], dvb)
            dhk = proj_nt(dvb, wv_f, addend=proj_nt(dk_pre, wk_f, name="proj_dh"), name="proj_dh2")
            dx, g_kvnorm = rms_bwd(kv_saved["x"], kv_norm[None], dhk, dx)
            outbox.append((_Exchange("scatter", [g_wk, g_wv] + g_wq + g_wo, rows=ds_),
                           functools.partial(mix_land.__setitem__, "att")))
    grad_x = dx[None]

    att_g = [sum_slots(z, "sum_mix") for z in mix_land["att"]]
    grad_w_k, grad_w_v = att_g[0], att_g[1]
    grad_w_q = jnp.stack(att_g[2:2 + n_b])
    grad_w_o = jnp.stack(att_g[2 + n_b:2 + 2 * n_b])
    grad_pool_w = jnp.stack([sum_slots(mix_land["pool", l][0], "sum_pool").reshape(n_g, cg // N_DEV, cg)
                             for l in range(N_A_LAYERS)])

    small_shapes = [(depth, d), (depth, d), (depth, d), (d,), (N_A_LAYERS, d), k_gain.shape, q_gain.shape, rel_bias.shape]
    small_part = _pack_small(d, [jnp.concatenate(g_norm[0]), jnp.concatenate(g_norm[1]), jnp.concatenate(g_norm[2]),
                                 g_kvnorm, jnp.concatenate(g_pscale), g_kgain, jnp.concatenate(g_qgain), jnp.stack(g_bias)])
    rows_small = small_part.shape[0]
    (small_all,), = run_exchanges([_Exchange("gather", [small_part])], "gather_small")
    small_g = sum_slots(small_all.reshape(N_DEV, rows_small, d), "sum_small")
    grad_ffn1_norm, grad_mix_norm, grad_ffn2_norm, grad_kv_norm, g_pscale_full, grad_k_gain, grad_q_gain, grad_rel_bias = \
        _unpack_small(small_g, small_shapes)
    grad_pool_scale = lax.dynamic_slice_in_dim(g_pscale_full, me * ds_, ds_, axis=1)

    def packed(ps):
        return _pack_small(d, [ps[0], ps[1], ps[2], ps[3], jnp.zeros((N_A_LAYERS, d), F32), ps[4], ps[5], ps[6]])

    small_w = (ffn1_norm, mix_norm, ffn2_norm, kv_norm, k_gain, q_gain, rel_bias)
    small_m = (m_ffn1_norm, m_mix_norm, m_ffn2_norm, m_kv_norm, m_k_gain, m_q_gain, m_rel_bias)
    small_v = (v_ffn1_norm, v_mix_norm, v_ffn2_norm, v_kv_norm, v_k_gain, v_q_gain, v_rel_bias)
    small_out = [_unpack_small(z, small_shapes) for z in adamw(packed(small_w), packed(small_m), packed(small_v), small_g, "adamw_small")]

    def ffn_grad(s, j):
        return jnp.stack([g_ffn[s, l][j] for l in range(depth)])

    grads = {
        "ffn1_w_gate": ffn_grad(0, 0), "ffn1_w_up": ffn_grad(0, 1), "ffn1_w_down": ffn_grad(0, 2),
        "ffn2_w_gate": ffn_grad(1, 0), "ffn2_w_up": ffn_grad(1, 1), "ffn2_w_down": ffn_grad(1, 2),
        "pool_w": grad_pool_w, "pool_scale": grad_pool_scale, "w_k": grad_w_k, "w_v": grad_w_v,
        "w_q": grad_w_q, "w_o": grad_w_o,
    }
    given = dict(
        ffn1_w_gate=(ffn1_w_gate, m_ffn1_w_gate, v_ffn1_w_gate), ffn1_w_up=(ffn1_w_up, m_ffn1_w_up, v_ffn1_w_up),
        ffn1_w_down=(ffn1_w_down, m_ffn1_w_down, v_ffn1_w_down), ffn2_w_gate=(ffn2_w_gate, m_ffn2_w_gate, v_ffn2_w_gate),
        ffn2_w_up=(ffn2_w_up, m_ffn2_w_up, v_ffn2_w_up), ffn2_w_down=(ffn2_w_down, m_ffn2_w_down, v_ffn2_w_down),
        pool_w=(pool_w, m_pool_w, v_pool_w), pool_scale=(pool_scale, m_pool_scale, v_pool_scale),
        w_k=(w_k, m_w_k, v_w_k), w_v=(w_v, m_w_v, v_w_v), w_q=(w_q, m_w_q, v_w_q), w_o=(w_o, m_w_o, v_w_o),
    )
    upd = {n: _adamw_nd(*given[n], grads[n]) for n in grads}
    small_names = ("ffn1_norm", "mix_norm", "ffn2_norm", "kv_norm", None, "k_gain", "q_gain", "rel_bias")
    for j, n in enumerate(small_names):
        if n is not None:
            upd[n] = tuple(small_out[z][j] for z in range(3))
    grads.update(ffn1_norm=grad_ffn1_norm, mix_norm=grad_mix_norm, ffn2_norm=grad_ffn2_norm, kv_norm=grad_kv_norm,
                 k_gain=grad_k_gain, q_gain=grad_q_gain, rel_bias=grad_rel_bias)

    order = ("ffn1_norm", "ffn1_w_gate", "ffn1_w_up", "ffn1_w_down", "mix_norm", "ffn2_norm", "ffn2_w_gate", "ffn2_w_up",
             "ffn2_w_down", "pool_w", "pool_scale", "kv_norm", "w_k", "w_v", "k_gain", "w_q", "q_gain", "rel_bias", "w_o")
    return (loss, grad_x, *[grads[n] for n in order], *[upd[n][0] for n in order], *[upd[n][1] for n in order],
            *[upd[n][2] for n in order])
```

```python
import functools

import jax
import jax.numpy as jnp
from jax import lax
from jax.experimental import pallas as pl
from jax.experimental.pallas import tpu as pltpu

F32 = jnp.float32
BF16 = jnp.bfloat16
MESH = pl.DeviceIdType.MESH

EPS = 1e-6
NEG_INF = -1e30
CHUNK = 64
LEFT = 512
REL_MAX = 128
N_REL = (CHUNK - 1) + REL_MAX + 1
POOL_WINDOWS = (2, 4, 8, 16)
N_A_LAYERS = 2
HEAD_DIM = 128
HALO = 16

ADAM_LR, ADAM_B1, ADAM_B2, ADAM_EPS, ADAM_WD, ADAM_STEP = 0.001, 0.9, 0.999, 1e-08, 0.01, 10

N_DEV = 8
AXES = ("x", "y", "c")
VMEM_LIMIT_BYTES = 56 * 1024 * 1024
TM = 512
TM_FFN_FWD = 1024
TM_FFN_BWD_W = 1024
TK_GRAD = 2048
TF = 512
BQ = 256
QB_STEP = 8
ROW_CHUNK = 128
SUM_BLOCK_BYTES = 4 * 1024 * 1024
PROJ_CHUNK = 512
FF_CHUNK = 256


def _cparams(*sem):
    return pltpu.CompilerParams(dimension_semantics=sem, vmem_limit_bytes=VMEM_LIMIT_BYTES)


def _nn(a, b):
    return jnp.dot(a, b, preferred_element_type=F32)


def _nt(a, b):
    return lax.dot_general(a, b, (((1,), (1,)), ((), ())), preferred_element_type=F32)


def _tn(a, b):
    return lax.dot_general(a, b, (((0,), (0,)), ((), ())), preferred_element_type=F32)


def _rms_fwd(x, g):
    r = lax.rsqrt(jnp.mean(x * x, axis=-1, keepdims=True) + EPS)
    return x * r * g


def _rms_bwd(x, g, dh):
    r = lax.rsqrt(jnp.mean(x * x, axis=-1, keepdims=True) + EPS)
    xn = x * r
    dxn = dh * g
    dx = r * (dxn - xn * jnp.mean(dxn * xn, axis=-1, keepdims=True))
    return dx, jnp.sum(dh * xn, axis=0, keepdims=True)


def _silu_parts(g):
    sg = jax.nn.sigmoid(g)
    return g * sg, sg * (1.0 + g * (1.0 - sg))


def _row_tile(rows, dtype=F32, cap=512):
    mult = 8 * 4 // jnp.dtype(dtype).itemsize
    for t in range(min(rows, cap), mult - 1, -1):
        if rows % t == 0 and t % mult == 0:
            return t
    return rows


_PEER_FLIPS = ((0, 0, 1), (1, 0, 0), (0, 1, 0), (1, 0, 1), (0, 1, 1), (1, 1, 0), (1, 1, 1))


def _me_and_peers():
    x, y, c = (lax.axis_index(a) for a in AXES)
    peers = []
    for fx, fy, fc in _PEER_FLIPS:
        px, py, pc = (1 - x if fx else x), (1 - y if fy else y), (1 - c if fc else c)
        peers.append(((px, py, pc), 4 * px + 2 * py + pc))
    return 4 * x + 2 * y + c, peers


class _Exchange:
    def __init__(self, kind, arrs, rows=None, pad_rows=0, two_level=False):
        self.gather, self.n, self.pad = kind == "gather", len(arrs), pad_rows
        self.two_level = two_level and self.gather
        self.rows = [a.shape[0] if self.gather else rows for a in arrs]
        zeros = [jnp.zeros((pad_rows, a.shape[1]), a.dtype) for a in arrs] if pad_rows else []
        self.operands = list(arrs) + zeros
        if self.gather:
            self.out_shape = [jax.ShapeDtypeStruct((N_DEV * a.shape[0] + pad_rows, a.shape[1]), a.dtype) for a in arrs]
        else:
            self.out_shape = [jax.ShapeDtypeStruct((N_DEV, rows, a.shape[1]), a.dtype) for a in arrs]
        n_remote = self.n * len(_PEER_FLIPS)
        self.scratch = [pltpu.SemaphoreType.DMA((n_remote,)), pltpu.SemaphoreType.DMA((n_remote,)),
                        pltpu.SemaphoreType.DMA((2 * self.n,))]

    def _copies(self, ins, outs, sems):
        send, recv, loc = sems
        me, peers = _me_and_peers()

        def blk(ref, idx, r):
            return ref.at[pl.ds(pl.multiple_of(idx * r, 8), r)]

        local, sends, recvs = [], [], []
        for a in range(self.n):
            r = self.rows[a]
            if self.gather:
                local.append(pltpu.make_async_copy(ins[a], blk(outs[a], me, r), loc.at[2 * a]))
                if self.pad:
                    local.append(pltpu.make_async_copy(ins[self.n + a], outs[a].at[pl.ds(N_DEV * r, self.pad)],
                                                       loc.at[2 * a + 1]))
            else:
                local.append(pltpu.make_async_copy(blk(ins[a], me, r), outs[a].at[me], loc.at[2 * a]))
        for p, (dev, idx) in enumerate(peers):
            for a in range(self.n):
                r = self.rows[a]
                k = a * len(_PEER_FLIPS) + p
                if self.gather:
                    src, there, here = ins[a], blk(outs[a], me, r), blk(outs[a], idx, r)
                else:
                    src, there, here = blk(ins[a], idx, r), outs[a].at[me], outs[a].at[idx]
                sends.append(pltpu.make_async_remote_copy(src, there, send.at[k], recv.at[k], device_id=dev,
                                                          device_id_type=MESH))
                recvs.append(pltpu.make_async_remote_copy(src, here, send.at[k], recv.at[k], device_id=dev,
                                                          device_id_type=MESH))
        return local, sends, recvs

    def _copies_two_level(self, ins, outs, sems):
        send, recv, loc = sems
        x, y, c = (lax.axis_index(a) for a in AXES)
        me, sib = 4 * x + 2 * y + c, 4 * x + 2 * y + (1 - c)
        chips = [(1 - x, y), (x, 1 - y), (1 - x, 1 - y)]

        def blk(ref, idx, r):
            return ref.at[pl.ds(pl.multiple_of(idx * r, 8), r)]

        def remote(src, dst, k, dev):
            return pltpu.make_async_remote_copy(src, dst, send.at[k], recv.at[k], device_id=dev, device_id_type=MESH)

        local, first, passed = [], [], []
        for a in range(self.n):
            r = self.rows[a]
            local.append(pltpu.make_async_copy(ins[a], blk(outs[a], me, r), loc.at[2 * a]))
            if self.pad:
                local.append(pltpu.make_async_copy(ins[self.n + a], outs[a].at[pl.ds(N_DEV * r, self.pad)], loc.at[2 * a + 1]))
        k0 = len(_PEER_FLIPS)
        for a in range(self.n):
            r = self.rows[a]
            first.append((remote(ins[a], blk(outs[a], me, r), a * k0, (x, y, 1 - c)),
                          remote(ins[a], blk(outs[a], sib, r), a * k0, (x, y, 1 - c))))
        for j, (cx, cy) in enumerate(chips):
            for a in range(self.n):
                r = self.rows[a]
                theirs, sib_theirs = 4 * cx + 2 * cy + c, 4 * cx + 2 * cy + (1 - c)
                first.append((remote(ins[a], blk(outs[a], me, r), a * k0 + 1 + j, (cx, cy, c)), None))
                passed.append((remote(ins[a], blk(outs[a], theirs, r), a * k0 + 1 + j, (cx, cy, c)),
                               remote(blk(outs[a], theirs, r), blk(outs[a], theirs, r), a * k0 + 4 + j, (x, y, 1 - c)),
                               remote(blk(outs[a], theirs, r), blk(outs[a], sib_theirs, r), a * k0 + 4 + j, (x, y, 1 - c))))
        return local, first, passed

    def start(self, ins, outs, sems):
        if self.two_level:
            local, first, _ = self._copies_two_level(ins, outs, sems)
            for cp in local + [s for s, _ in first]:
                cp.start()
            return
        local, sends, _ = self._copies(ins, outs, sems)
        for cp in local + sends:
            cp.start()

    def middle(self, ins, outs, sems):
        if self.two_level:
            _, _, passed = self._copies_two_level(ins, outs, sems)
            for arrival, pass_on, _ in passed:
                arrival.wait_recv()
                pass_on.start()

    def wait(self, ins, outs, sems):
        if self.two_level:
            local, first, passed = self._copies_two_level(ins, outs, sems)
            for cp in local:
                cp.wait()
            for s, r in first:
                s.wait_send()
                if r is not None:
                    r.wait_recv()
            for _, pass_on, from_sibling in passed:
                pass_on.wait_send()
                from_sibling.wait_recv()
            return
        local, sends, recvs = self._copies(ins, outs, sems)
        for cp in local:
            cp.wait()
        for cp in sends:
            cp.wait_send()
        for cp in recvs:
            cp.wait_recv()


def _host_call(body, operands, *, grid, in_specs, out_specs, out_shape, scratch_shapes=(), semantics, name, carry=()):
    n_in, n_out, n_scr = len(in_specs), len(out_specs), len(scratch_shapes)

    def wrapped(*refs):
        pos = [0]

        def take(k):
            pos[0] += k
            return refs[pos[0] - k:pos[0]]

        ins, c_ins = take(n_in), [take(len(e.operands)) for e in carry]
        outs, c_outs = take(n_out), [take(e.n) for e in carry]
        scr, c_scr = take(n_scr), [take(len(e.scratch)) for e in carry]
        ids = [pl.program_id(a) for a in range(len(grid))]
        if carry:
            @pl.when(functools.reduce(jnp.logical_and, [i == 0 for i in ids]))
            def _():
                for e, ci, co, cs in zip(carry, c_ins, c_outs, c_scr):
                    e.start(ci, co, cs)

        if any(e.two_level for e in carry):
            @pl.when(functools.reduce(jnp.logical_and, [ids[0] == (3 * grid[0]) // 4] + [i == 0 for i in ids[1:]]))
            def _():
                for e, ci, co, cs in zip(carry, c_ins, c_outs, c_scr):
                    e.middle(ci, co, cs)

        body(*ins, *outs, *scr)
        if carry:
            @pl.when(functools.reduce(jnp.logical_and, [i == g - 1 for i, g in zip(ids, grid)]))
            def _():
                for e, ci, co, cs in zip(carry, c_ins, c_outs, c_scr):
                    e.wait(ci, co, cs)

    any_spec = pl.BlockSpec(memory_space=pl.ANY)
    res = pl.pallas_call(
        wrapped, grid=grid,
        in_specs=list(in_specs) + [any_spec] * sum(len(e.operands) for e in carry),
        out_specs=list(out_specs) + [any_spec] * sum(e.n for e in carry),
        out_shape=list(out_shape) + [s for e in carry for s in e.out_shape],
        scratch_shapes=list(scratch_shapes) + [s for e in carry for s in e.scratch],
        compiler_params=_cparams(*(["arbitrary"] * len(grid) if carry else semantics)), name=name,
    )(*operands, *[o for e in carry for o in e.operands])
    host, rest, carried = res[:n_out], res[n_out:], []
    for e in carry:
        carried.append(rest[:e.n])
        rest = rest[e.n:]
    return host, carried


def run_exchanges(exchanges, name):
    def body(*refs):
        pos = [0]

        def take(k):
            pos[0] += k
            return refs[pos[0] - k:pos[0]]

        c_ins = [take(len(e.operands)) for e in exchanges]
        c_outs = [take(e.n) for e in exchanges]
        c_scr = [take(len(e.scratch)) for e in exchanges]
        for e, ci, co, cs in zip(exchanges, c_ins, c_outs, c_scr):
            e.start(ci, co, cs)
        for e, ci, co, cs in zip(exchanges, c_ins, c_outs, c_scr):
            e.middle(ci, co, cs)
        for e, ci, co, cs in zip(exchanges, c_ins, c_outs, c_scr):
            e.wait(ci, co, cs)

    any_spec = pl.BlockSpec(memory_space=pl.ANY)
    res = pl.pallas_call(
        body,
        out_shape=[s for e in exchanges for s in e.out_shape],
        in_specs=[any_spec] * sum(len(e.operands) for e in exchanges),
        out_specs=[any_spec] * sum(e.n for e in exchanges),
        scratch_shapes=[s for e in exchanges for s in e.scratch],
        name=name,
    )(*[o for e in exchanges for o in e.operands])
    out = []
    for e in exchanges:
        out.append(res[:e.n])
        res = res[e.n:]
    return out


def sum_slots(land, name="sum_slots", transpose=False):
    _, rows, cols = land.shape
    tc = cols
    while tc % 256 == 0 and N_DEV * rows * tc * land.dtype.itemsize > SUM_BLOCK_BYTES:
        tc //= 2

    def body(l_ref, o_ref):
        acc = l_ref[0].astype(F32)
        for s in range(1, N_DEV):
            acc = acc + l_ref[s].astype(F32)
        o_ref[...] = acc.T if transpose else acc

    return pl.pallas_call(
        body, grid=(cols // tc,),
        in_specs=[pl.BlockSpec((N_DEV, rows, tc), lambda i: (0, 0, i))],
        out_specs=pl.BlockSpec((tc, rows), lambda i: (i, 0)) if transpose else pl.BlockSpec((rows, tc), lambda i: (0, i)),
        out_shape=jax.ShapeDtypeStruct((cols, rows) if transpose else (rows, cols), F32),
        compiler_params=_cparams("parallel"), name=name,
    )(land)


def cast_layer(w, l, transpose):
    _, rows, cols = w.shape

    def body(w_ref, o_ref):
        o_ref[...] = (w_ref[...].T if transpose else w_ref[...]).astype(BF16)

    out = (cols, rows) if transpose else (rows, cols)
    return pl.pallas_call(
        body, grid=(1,),
        in_specs=[pl.BlockSpec((None, rows, cols), lambda i: (l, 0, 0))],
        out_specs=pl.BlockSpec(out, lambda i: (0, 0)),
        out_shape=jax.ShapeDtypeStruct(out, BF16),
        compiler_params=_cparams("arbitrary"), name="cast_layer_t" if transpose else "cast_layer",
    )(w)


def ffn_fwd(x, gain, wg_t, wu_t, wd, carry=()):
    t, d = x.shape
    fp = wd.shape[0]
    tm = min(TM_FFN_FWD, t)
    n_t, n_f = t // tm, fp // TF

    def body(x_ref, g_ref, wg_ref, wu_ref, wd_ref, xo_ref, hn_ref, gate_ref, up_ref, act_ref):
        k = pl.program_id(1)

        @pl.when(k == 0)
        def _():
            for c in range(tm // ROW_CHUNK):
                rs = slice(c * ROW_CHUNK, (c + 1) * ROW_CHUNK)
                hn_ref[rs, :] = _rms_fwd(x_ref[rs, :], g_ref[...]).astype(BF16)
            xo_ref[...] = jnp.zeros_like(xo_ref)

        h = hn_ref[...]
        for c in range(TF // FF_CHUNK):
            cs = slice(c * FF_CHUNK, (c + 1) * FF_CHUNK)
            g = _nt(h, wg_ref[cs, :])
            u = _nt(h, wu_ref[cs, :])
            gate_ref[:, cs] = g.astype(BF16)
            up_ref[:, cs] = u.astype(BF16)
            act_ref[:, cs] = (g * jax.nn.sigmoid(g) * u).astype(BF16)
        xo_ref[...] += _nn(act_ref[...], wd_ref[...])

        @pl.when(k == n_f - 1)
        def _():
            xo_ref[...] = x_ref[...] + 0.5 * xo_ref[...]

    tok = pl.BlockSpec((tm, d), lambda i, k: (i, 0))
    tok_once = pl.BlockSpec((tm, d), lambda i, k: (i, 0), pipeline_mode=pl.Buffered(1))
    wsp = pl.BlockSpec((TF, d), lambda i, k: (k, 0))
    hid = pl.BlockSpec((tm, TF), lambda i, k: (i, k))
    return _host_call(
        body, (x, gain, wg_t, wu_t, wd), grid=(n_t, n_f),
        in_specs=[tok_once, pl.BlockSpec((1, d), lambda i, k: (0, 0)), wsp, wsp, wsp],
        out_specs=[tok, tok, hid, hid],
        out_shape=[jax.ShapeDtypeStruct((t, d), F32), jax.ShapeDtypeStruct((t, d), BF16),
                   jax.ShapeDtypeStruct((t, fp), BF16), jax.ShapeDtypeStruct((t, fp), BF16)],
        scratch_shapes=[pltpu.VMEM((tm, TF), BF16)],
        semantics=("parallel", "arbitrary"), name="ffn_fwd", carry=carry,
    )


def ffn_bwd_x(dxo, x, gain, gate, up, wg_t, wu_t, wd, carry=()):
    t, d = x.shape
    fp = wd.shape[0]
    n_t, n_f = t // TM, fp // TF

    def body(dxo_ref, x_ref, g_ref, gate_ref, up_ref, wg_ref, wu_ref, wd_ref,
             dxi_ref, dgate_ref, dup_ref, dy_ref, dgain_ref):
        i, k = pl.program_id(0), pl.program_id(1)

        @pl.when(k == 0)
        def _():
            dy_ref[...] = (0.5 * dxo_ref[...]).astype(BF16)
            dxi_ref[...] = jnp.zeros_like(dxi_ref)

        @pl.when(jnp.logical_and(i == 0, k == 0))
        def _():
            dgain_ref[...] = jnp.zeros_like(dgain_ref)

        dy = dy_ref[...]
        for c in range(TF // FF_CHUNK):
            cs = slice(c * FF_CHUNK, (c + 1) * FF_CHUNK)
            dact = _nt(dy, wd_ref[cs, :])
            silu, dsilu = _silu_parts(gate_ref[:, cs].astype(F32))
            dgate_ref[:, cs] = (dact * up_ref[:, cs].astype(F32) * dsilu).astype(BF16)
            dup_ref[:, cs] = (dact * silu).astype(BF16)
        dxi_ref[...] += _nn(dgate_ref[...], wg_ref[...]) + _nn(dup_ref[...], wu_ref[...])

        @pl.when(k == n_f - 1)
        def _():
            for c in range(TM // ROW_CHUNK):
                rs = slice(c * ROW_CHUNK, (c + 1) * ROW_CHUNK)
                dx, dg = _rms_bwd(x_ref[rs, :], g_ref[...], dxi_ref[rs, :])
                dxi_ref[rs, :] = dxo_ref[rs, :] + dx
                dgain_ref[...] += dg

    tok = pl.BlockSpec((TM, d), lambda i, k: (i, 0))
    row = pl.BlockSpec((1, d), lambda i, k: (0, 0))
    wsp = pl.BlockSpec((TF, d), lambda i, k: (k, 0))
    hid = pl.BlockSpec((TM, TF), lambda i, k: (i, k))
    return _host_call(
        body, (dxo, x, gain, gate, up, wg_t, wu_t, wd), grid=(n_t, n_f),
        in_specs=[tok, tok, row, hid, hid, wsp, wsp, wsp],
        out_specs=[tok, hid, hid, tok, row],
        out_shape=[jax.ShapeDtypeStruct((t, d), F32), jax.ShapeDtypeStruct((t, fp), BF16),
                   jax.ShapeDtypeStruct((t, fp), BF16), jax.ShapeDtypeStruct((t, d), BF16),
                   jax.ShapeDtypeStruct((1, d), F32)],
        semantics=("arbitrary", "arbitrary"), name="ffn_bwd_x", carry=carry,
    )


def ffn_bwd_hidden(dxo, gate, up, wd):
    t, d = dxo.shape
    fp = wd.shape[0]
    n_t, n_f = t // TM, fp // TF

    def body(dxo_ref, gate_ref, up_ref, wd_ref, dgate_ref, dup_ref, dy_ref):
        @pl.when(pl.program_id(1) == 0)
        def _():
            dy_ref[...] = (0.5 * dxo_ref[...]).astype(BF16)

        dy = dy_ref[...]
        for c in range(TF // FF_CHUNK):
            cs = slice(c * FF_CHUNK, (c + 1) * FF_CHUNK)
            dact = _nt(dy, wd_ref[cs, :])
            silu, dsilu = _silu_parts(gate_ref[:, cs].astype(F32))
            dgate_ref[:, cs] = (dact * up_ref[:, cs].astype(F32) * dsilu).astype(BF16)
            dup_ref[:, cs] = (dact * silu).astype(BF16)

    tok = pl.BlockSpec((TM, d), lambda i, k: (i, 0))
    hid = pl.BlockSpec((TM, TF), lambda i, k: (i, k))
    return pl.pallas_call(
        body, grid=(n_t, n_f),
        in_specs=[tok, hid, hid, pl.BlockSpec((TF, d), lambda i, k: (k, 0))],
        out_specs=[hid, hid, tok],
        out_shape=[jax.ShapeDtypeStruct((t, fp), BF16), jax.ShapeDtypeStruct((t, fp), BF16),
                   jax.ShapeDtypeStruct((t, d), BF16)],
        compiler_params=_cparams("parallel", "arbitrary"), name="ffn_bwd_hidden",
    )(dxo, gate, up, wd)


def ffn_bwd_input(dxo, x, gain, dgate, dup, wg_t, wu_t, carry=()):
    t, d = x.shape
    fp = wg_t.shape[0]
    n_t, n_f = t // TM, fp // TF

    def body(dxo_ref, x_ref, g_ref, dgate_ref, dup_ref, wg_ref, wu_ref, dxi_ref, dgain_ref):
        i, k = pl.program_id(0), pl.program_id(1)

        @pl.when(k == 0)
        def _():
            dxi_ref[...] = jnp.zeros_like(dxi_ref)

        @pl.when(jnp.logical_and(i == 0, k == 0))
        def _():
            dgain_ref[...] = jnp.zeros_like(dgain_ref)

        dxi_ref[...] += _nn(dgate_ref[...], wg_ref[...]) + _nn(dup_ref[...], wu_ref[...])

        @pl.when(k == n_f - 1)
        def _():
            for c in range(TM // ROW_CHUNK):
                rs = slice(c * ROW_CHUNK, (c + 1) * ROW_CHUNK)
                dx, dg = _rms_bwd(x_ref[rs, :], g_ref[...], dxi_ref[rs, :])
                dxi_ref[rs, :] = dxo_ref[rs, :] + dx
                dgain_ref[...] += dg

    tok = pl.BlockSpec((TM, d), lambda i, k: (i, 0))
    row = pl.BlockSpec((1, d), lambda i, k: (0, 0))
    wsp = pl.BlockSpec((TF, d), lambda i, k: (k, 0))
    hid = pl.BlockSpec((TM, TF), lambda i, k: (i, k))
    return _host_call(
        body, (dxo, x, gain, dgate, dup, wg_t, wu_t), grid=(n_t, n_f),
        in_specs=[tok, tok, row, hid, hid, wsp, wsp],
        out_specs=[tok, row],
        out_shape=[jax.ShapeDtypeStruct((t, d), F32), jax.ShapeDtypeStruct((1, d), F32)],
        semantics=("arbitrary", "arbitrary"), name="ffn_bwd_input", carry=carry,
    )


def ffn_bwd_w(hn, dy, gate, up, dgate, dup, carry=()):
    t, d = hn.shape
    fp = gate.shape[1]
    tm = min(TM_FFN_BWD_W, t)
    n_t, n_f = t // tm, fp // TF

    def body(hn_ref, dy_ref, gate_ref, up_ref, dgate_ref, dup_ref, dwg_ref, dwu_ref, dwd_ref, ag_ref, au_ref, ad_ref):
        i = pl.program_id(1)

        @pl.when(i == 0)
        def _():
            ag_ref[...] = jnp.zeros_like(ag_ref)
            au_ref[...] = jnp.zeros_like(au_ref)
            ad_ref[...] = jnp.zeros_like(ad_ref)

        h = hn_ref[...]
        ag_ref[...] += _tn(dgate_ref[...], h)
        au_ref[...] += _tn(dup_ref[...], h)
        silu, _ = _silu_parts(gate_ref[...].astype(F32))
        act = (silu * up_ref[...].astype(F32)).astype(BF16)
        ad_ref[...] += _tn(act, dy_ref[...])

        @pl.when(i == n_t - 1)
        def _():
            dwg_ref[...] = ag_ref[...].astype(BF16)
            dwu_ref[...] = au_ref[...].astype(BF16)
            dwd_ref[...] = ad_ref[...].astype(BF16)

    tok = pl.BlockSpec((tm, d), lambda k, i: (i, 0))
    hid = pl.BlockSpec((tm, TF), lambda k, i: (i, k))
    wsp = pl.BlockSpec((TF, d), lambda k, i: (k, 0))
    return _host_call(
        body, (hn, dy, gate, up, dgate, dup), grid=(n_f, n_t),
        in_specs=[tok, tok, hid, hid, hid, hid],
        out_specs=[wsp, wsp, wsp],
        out_shape=[jax.ShapeDtypeStruct((fp, d), BF16)] * 3,
        scratch_shapes=[pltpu.VMEM((TF, d), F32)] * 3,
        semantics=("parallel", "arbitrary"), name="ffn_bwd_w", carry=carry,
    )


def rms_fwd(x, gain):
    t, d = x.shape

    def body(x_ref, g_ref, o_ref):
        o_ref[...] = _rms_fwd(x_ref[...], g_ref[...]).astype(BF16)

    tok = pl.BlockSpec((TM, d), lambda i: (i, 0))
    return pl.pallas_call(
        body, grid=(t // TM,), in_specs=[tok, pl.BlockSpec((1, d), lambda i: (0, 0))], out_specs=tok,
        out_shape=jax.ShapeDtypeStruct((t, d), BF16), compiler_params=_cparams("parallel"), name="rms_fwd",
    )(x, gain)


def rms_bwd(x, gain, dh, dx_in):
    t, d = x.shape

    def body(x_ref, g_ref, dh_ref, dxi_ref, dxo_ref, dg_ref):
        i = pl.program_id(0)

        @pl.when(i == 0)
        def _():
            dg_ref[...] = jnp.zeros_like(dg_ref)

        dx, dg = _rms_bwd(x_ref[...], g_ref[...], dh_ref[...])
        dxo_ref[...] = dxi_ref[...] + dx
        dg_ref[...] += dg

    tok = pl.BlockSpec((TM, d), lambda i: (i, 0))
    row = pl.BlockSpec((1, d), lambda i: (0, 0))
    return pl.pallas_call(
        body, grid=(t // TM,), in_specs=[tok, row, tok, tok], out_specs=[tok, row],
        out_shape=[jax.ShapeDtypeStruct((t, d), F32), jax.ShapeDtypeStruct((1, d), F32)],
        compiler_params=_cparams("arbitrary"), name="rms_bwd",
    )(x, gain, dh, dx_in)


def proj_nn(a, w, *, head_gain=None, residual=None, out_dtype=F32, name="proj_nn"):
    t, kd = a.shape
    n = w.shape[1]
    tn = min(PROJ_CHUNK, n)

    def body(*refs):
        a_ref, w_ref = refs[:2]
        for c in range(n // tn):
            cs = slice(c * tn, (c + 1) * tn)
            acc = _nn(a_ref[...], w_ref[:, cs])
            if head_gain is not None:
                g_ref, o_ref, on_ref = refs[2:]
                o_ref[:, cs] = acc
                for j in range(tn // HEAD_DIM):
                    sl = slice(j * HEAD_DIM, (j + 1) * HEAD_DIM)
                    on_ref[:, c * tn + j * HEAD_DIM:c * tn + (j + 1) * HEAD_DIM] = _rms_fwd(acc[:, sl], g_ref[...]).astype(BF16)
            elif residual is not None:
                r_ref, o_ref = refs[2:]
                o_ref[:, cs] = r_ref[:, cs] + acc
            else:
                refs[2][:, cs] = acc.astype(out_dtype)

    a_spec = pl.BlockSpec((TM, kd), lambda i: (i, 0))
    w_spec = pl.BlockSpec((kd, n), lambda i: (0, 0), pipeline_mode=pl.Buffered(1))
    o_spec = pl.BlockSpec((TM, n), lambda i: (i, 0))
    ins, in_specs = [a, w], [a_spec, w_spec]
    if head_gain is not None:
        ins.append(head_gain)
        in_specs.append(pl.BlockSpec((1, HEAD_DIM), lambda i: (0, 0)))
        out_specs = [o_spec, o_spec]
        out_shape = [jax.ShapeDtypeStruct((t, n), F32), jax.ShapeDtypeStruct((t, n), BF16)]
    elif residual is not None:
        ins.append(residual)
        in_specs.append(o_spec)
        out_specs, out_shape = o_spec, jax.ShapeDtypeStruct((t, n), F32)
    else:
        out_specs, out_shape = o_spec, jax.ShapeDtypeStruct((t, n), out_dtype)
    return pl.pallas_call(
        body, grid=(t // TM,), in_specs=in_specs, out_specs=out_specs, out_shape=out_shape,
        compiler_params=_cparams("parallel"), name=name,
    )(*ins)


def proj_nt(a, w, *, addend=None, out_dtype=F32, name="proj_nt"):
    t, n = a.shape
    kd = w.shape[0]
    tk = min(PROJ_CHUNK, kd)

    def body(*refs):
        a_ref, w_ref = refs[:2]
        for c in range(kd // tk):
            cs = slice(c * tk, (c + 1) * tk)
            acc = _nt(a_ref[...], w_ref[cs, :])
            if addend is not None:
                acc = acc + refs[2][:, cs]
            refs[-1][:, cs] = acc.astype(out_dtype)

    o_spec = pl.BlockSpec((TM, kd), lambda i: (i, 0))
    ins = [a, w] + ([addend] if addend is not None else [])
    in_specs = [pl.BlockSpec((TM, n), lambda i: (i, 0)),
                pl.BlockSpec((kd, n), lambda i: (0, 0), pipeline_mode=pl.Buffered(1))]
    if addend is not None:
        in_specs.append(o_spec)
    return pl.pallas_call(
        body, grid=(t // TM,), in_specs=in_specs, out_specs=o_spec,
        out_shape=jax.ShapeDtypeStruct((t, kd), out_dtype),
        compiler_params=_cparams("parallel"), name=name,
    )(*ins)


def grad_tn(a, b, name="grad_tn"):
    t, m = a.shape
    n = b.shape[1]
    tmo, tno = min(1024, m), min(1024, n)
    tk = min(TK_GRAD, t)
    n_t = t // tk

    def body(a_ref, b_ref, o_ref, acc_ref):
        k = pl.program_id(2)

        @pl.when(k == 0)
        def _():
            acc_ref[...] = jnp.zeros_like(acc_ref)

        acc_ref[...] += _tn(a_ref[...], b_ref[...])

        @pl.when(k == n_t - 1)
        def _():
            o_ref[...] = acc_ref[...].astype(BF16)

    return pl.pallas_call(
        body, grid=(m // tmo, n // tno, n_t),
        in_specs=[pl.BlockSpec((tk, tmo), lambda i, j, k: (k, i)), pl.BlockSpec((tk, tno), lambda i, j, k: (k, j))],
        out_specs=pl.BlockSpec((tmo, tno), lambda i, j, k: (i, j)),
        out_shape=jax.ShapeDtypeStruct((m, n), BF16),
        scratch_shapes=[pltpu.VMEM((tmo, tno), F32)],
        compiler_params=_cparams("parallel", "parallel", "arbitrary"), name=name,
    )(a, b)


def head_rms_bwd(dy, pre, gain):
    t, d = pre.shape

    def body(dy_ref, p_ref, g_ref, o_ref, dg_ref):
        i = pl.program_id(0)

        @pl.when(i == 0)
        def _():
            dg_ref[...] = jnp.zeros_like(dg_ref)

        dg_acc = jnp.zeros((1, HEAD_DIM), F32)
        for j in range(d // HEAD_DIM):
            sl = slice(j * HEAD_DIM, (j + 1) * HEAD_DIM)
            dx, dg = _rms_bwd(p_ref[:, sl], g_ref[...], dy_ref[:, sl])
            o_ref[:, sl] = dx.astype(BF16)
            dg_acc = dg_acc + dg
        dg_ref[...] += dg_acc

    tok = pl.BlockSpec((TM, d), lambda i: (i, 0))
    row = pl.BlockSpec((1, HEAD_DIM), lambda i: (0, 0))
    return pl.pallas_call(
        body, grid=(t // TM,), in_specs=[tok, tok, row], out_specs=[tok, row],
        out_shape=[jax.ShapeDtypeStruct((t, d), BF16), jax.ShapeDtypeStruct((1, HEAD_DIM), F32)],
        compiler_params=_cparams("arbitrary"), name="head_rms_bwd",
    )(dy, pre, gain)


def band_bias(table):
    h = table.shape[0]
    far = jnp.broadcast_to(table[:, N_REL - 1:], (h, LEFT - CHUNK))
    w = jnp.concatenate([far, jnp.flip(table[:, :N_REL - 1], axis=1)], axis=1)
    band = jnp.stack([w[:, CHUNK - 1 - r:CHUNK - 1 - r + LEFT + CHUNK] for r in range(CHUNK)], axis=1)
    out = jnp.full((h, BQ, BQ + LEFT), NEG_INF, F32)
    for a in range(BQ // CHUNK):
        out = out.at[:, a * CHUNK:(a + 1) * CHUNK, a * CHUNK:a * CHUNK + LEFT + CHUNK].set(band)
    return out


def _attn_probs(q, k, b_ref, start):
    w = k.shape[0]
    s = _nt(q, k) * (HEAD_DIM ** -0.5) + b_ref[...]
    kpos = start + lax.broadcasted_iota(jnp.int32, (1, w), 1)
    s = jnp.where(kpos >= LEFT, s, NEG_INF)
    e = jnp.exp(s - jnp.max(s, axis=-1, keepdims=True))
    return e / jnp.sum(e, axis=-1, keepdims=True)


def attn_fwd(qn, k_pad, v_pad, bias):
    t, d = qn.shape
    tp = k_pad.shape[0]
    w = BQ + LEFT
    qb = min(QB_STEP, t // BQ)

    def body(q_ref, k_ref, v_ref, b_ref, o_ref):
        i = pl.program_id(1)
        for sub in range(qb):
            rs = slice(sub * BQ, (sub + 1) * BQ)
            start = pl.multiple_of((i * qb + sub) * BQ, BQ)
            p = _attn_probs(q_ref[rs, :], k_ref[pl.ds(start, w), :], b_ref, start)
            o_ref[rs, :] = _nn(p.astype(BF16), v_ref[pl.ds(start, w), :]).astype(BF16)

    q_spec = pl.BlockSpec((qb * BQ, HEAD_DIM), lambda h, i: (i, h))
    kv_spec = pl.BlockSpec((tp, HEAD_DIM), lambda h, i: (0, h))
    return pl.pallas_call(
        body, grid=(d // HEAD_DIM, t // (qb * BQ)),
        in_specs=[q_spec, kv_spec, kv_spec, pl.BlockSpec((None, BQ, w), lambda h, i: (h, 0, 0))],
        out_specs=q_spec, out_shape=jax.ShapeDtypeStruct((t, d), BF16),
        compiler_params=_cparams("parallel", "parallel"), name="attn_fwd",
    )(qn, k_pad, v_pad, bias)


def attn_bwd(qn, k_pad, v_pad, bias, do):
    t, d = qn.shape
    tp = k_pad.shape[0]
    w = BQ + LEFT
    qb = min(QB_STEP, t // BQ)
    scale = HEAD_DIM ** -0.5

    def body(q_ref, k_ref, v_ref, b_ref, do_ref, dq_ref, dk_ref, dv_ref, db_ref):
        i = pl.program_id(1)

        @pl.when(i == 0)
        def _():
            dk_ref[...] = jnp.zeros_like(dk_ref)
            dv_ref[...] = jnp.zeros_like(dv_ref)
            db_ref[...] = jnp.zeros_like(db_ref)

        for sub in range(qb):
            rs = slice(sub * BQ, (sub + 1) * BQ)
            start = pl.multiple_of((i * qb + sub) * BQ, BQ)
            q, do = q_ref[rs, :], do_ref[rs, :]
            k = k_ref[pl.ds(start, w), :]
            v = v_ref[pl.ds(start, w), :]
            p = _attn_probs(q, k, b_ref, start)
            dp = _nt(do, v)
            ds = p * (dp - jnp.sum(p * dp, axis=-1, keepdims=True))
            db_ref[...] += ds
            dsb = ds.astype(BF16)
            dq_ref[rs, :] = _nn(dsb, k) * scale
            dk_ref[pl.ds(start, w), :] += _tn(dsb, q) * scale
            dv_ref[pl.ds(start, w), :] += _tn(p.astype(BF16), do)

    q_spec = pl.BlockSpec((qb * BQ, HEAD_DIM), lambda h, i: (i, h))
    kv_spec = pl.BlockSpec((tp, HEAD_DIM), lambda h, i: (0, h))
    b_spec = pl.BlockSpec((None, BQ, w), lambda h, i: (h, 0, 0))
    return pl.pallas_call(
        body, grid=(d // HEAD_DIM, t // (qb * BQ)),
        in_specs=[q_spec, kv_spec, kv_spec, b_spec, q_spec],
        out_specs=[q_spec, kv_spec, kv_spec, b_spec],
        out_shape=[jax.ShapeDtypeStruct((t, d), F32), jax.ShapeDtypeStruct((tp, d), F32),
                   jax.ShapeDtypeStruct((tp, d), F32), jax.ShapeDtypeStruct(bias.shape, F32)],
        compiler_params=_cparams("parallel", "arbitrary"), name="attn_bwd",
    )(qn, k_pad, v_pad, bias, do)


def _pool_diff(ext_ref, h, t0, g, cg):
    tm = h.shape[0]
    cs = slice(g * cg, (g + 1) * cg)
    win = POOL_WINDOWS[g]
    s = ext_ref[HALO:HALO + tm, cs]
    for j in range(1, win):
        s = s + ext_ref[HALO - j:HALO - j + tm, cs]
    cnt = jnp.minimum(t0 + lax.broadcasted_iota(jnp.int32, (tm, 1), 0) + 1, win).astype(F32)
    return (s / cnt - h[:, cs]).astype(BF16), cnt


def pool_fwd(x, gain, w, scale):
    t, d = x.shape
    n_g = len(POOL_WINDOWS)
    cg = d // n_g

    def body(x_ref, g_ref, w_ref, s_ref, o_ref, ext_ref):
        i = pl.program_id(0)
        h = _rms_fwd(x_ref[...], g_ref[...])

        @pl.when(i == 0)
        def _():
            ext_ref[0:HALO, :] = jnp.zeros((HALO, d), F32)

        @pl.when(i > 0)
        def _():
            ext_ref[0:HALO, :] = ext_ref[TM:TM + HALO, :]

        ext_ref[HALO:HALO + TM, :] = h
        for g in range(n_g):
            cs = slice(g * cg, (g + 1) * cg)
            diff, _ = _pool_diff(ext_ref, h, i * TM, g, cg)
            o_ref[:, cs] = x_ref[:, cs] + _nn(diff, w_ref[g]) * s_ref[:, cs]

    tok = pl.BlockSpec((TM, d), lambda i: (i, 0))
    row = pl.BlockSpec((1, d), lambda i: (0, 0))
    return pl.pallas_call(
        body, grid=(t // TM,),
        in_specs=[tok, row, pl.BlockSpec((n_g, cg, cg), lambda i: (0, 0, 0)), row],
        out_specs=tok, out_shape=jax.ShapeDtypeStruct((t, d), F32),
        scratch_shapes=[pltpu.VMEM((HALO + TM, d), F32)],
        compiler_params=_cparams("arbitrary"), name="pool_fwd",
    )(x, gain, w, scale)


def pool_bwd(x, dxo, gain, w, scale):
    t, d = x.shape
    n_g = len(POOL_WINDOWS)
    cg = d // n_g
    n_t = t // TM

    def body(x_ref, xh_ref, dxo_ref, g_ref, w_ref, s_ref, dxi_ref, dw_ref, ds_ref, dg_ref, exth_ref, extq_ref, dh_ref):
        i = pl.program_id(0)
        ti = n_t - 1 - i

        @pl.when(i == 0)
        def _():
            dw_ref[...] = jnp.zeros_like(dw_ref)
            ds_ref[...] = jnp.zeros_like(ds_ref)
            dg_ref[...] = jnp.zeros_like(dg_ref)
            extq_ref[TM:TM + HALO, :] = jnp.zeros((HALO, d), F32)

        @pl.when(i > 0)
        def _():
            extq_ref[TM:TM + HALO, :] = extq_ref[0:HALO, :]

        xv = x_ref[...]
        h = _rms_fwd(xv, g_ref[...])
        exth_ref[0:HALO, :] = jnp.where(ti > 0, _rms_fwd(xh_ref[...], g_ref[...]), 0.0)
        exth_ref[HALO:HALO + TM, :] = h
        for g in range(n_g):
            cs = slice(g * cg, (g + 1) * cg)
            diff, cnt = _pool_diff(exth_ref, h, ti * TM, g, cg)
            dxo_g = dxo_ref[:, cs]
            ds_ref[:, cs] += jnp.sum(dxo_g * _nn(diff, w_ref[g]), axis=0, keepdims=True)
            dy = (dxo_g * s_ref[:, cs]).astype(BF16)
            dw_ref[g] += _tn(diff, dy)
            ddiff = _nt(dy, w_ref[g])
            extq_ref[0:TM, cs] = ddiff / cnt
            acc = -ddiff
            for j in range(POOL_WINDOWS[g]):
                acc = acc + extq_ref[j:j + TM, cs]
            dh_ref[:, cs] = acc
        dx, dg = _rms_bwd(xv, g_ref[...], dh_ref[...])
        dxi_ref[...] = dxo_ref[...] + dx
        dg_ref[...] += dg

    tok = pl.BlockSpec((TM, d), lambda i: (n_t - 1 - i, 0))
    halo = pl.BlockSpec((HALO, d), lambda i: (jnp.maximum((n_t - 1 - i) * (TM // HALO) - 1, 0), 0))
    row = pl.BlockSpec((1, d), lambda i: (0, 0))
    wsp = pl.BlockSpec((n_g, cg, cg), lambda i: (0, 0, 0))
    return pl.pallas_call(
        body, grid=(n_t,),
        in_specs=[tok, halo, tok, row, wsp, row],
        out_specs=[tok, wsp, row, row],
        out_shape=[jax.ShapeDtypeStruct((t, d), F32), jax.ShapeDtypeStruct((n_g, cg, cg), F32),
                   jax.ShapeDtypeStruct((1, d), F32), jax.ShapeDtypeStruct((1, d), F32)],
        scratch_shapes=[pltpu.VMEM((HALO + TM, d), F32), pltpu.VMEM((TM + HALO, d), F32), pltpu.VMEM((TM, d), F32)],
        compiler_params=_cparams("arbitrary"), name="pool_bwd",
    )(x, x, dxo, gain, w, scale)


def loss_head(y, target):
    t, d = y.shape

    def body(y_ref, t_ref, sq_ref, dy_ref):
        i = pl.program_id(0)

        @pl.when(i == 0)
        def _():
            sq_ref[...] = jnp.zeros_like(sq_ref)

        err = y_ref[...] - t_ref[...]
        sq_ref[...] += jnp.sum(err * err, axis=0, keepdims=True)
        dy_ref[...] = err * (1.0 / d)

    tok = pl.BlockSpec((TM, d), lambda i: (i, 0))
    row = pl.BlockSpec((1, d), lambda i: (0, 0))
    return pl.pallas_call(
        body, grid=(t // TM,), in_specs=[tok, tok], out_specs=[row, tok],
        out_shape=[jax.ShapeDtypeStruct((1, d), F32), jax.ShapeDtypeStruct((t, d), F32)],
        compiler_params=_cparams("arbitrary"), name="loss_head",
    )(y, target)


def adamw(w, m, v, g, name="adamw"):
    rows, cols = w.shape
    tr = _row_tile(rows)

    def body(w_ref, m_ref, v_ref, g_ref, d_ref, mo_ref, vo_ref):
        gv = g_ref[...]
        mn = ADAM_B1 * m_ref[...] + (1.0 - ADAM_B1) * gv
        vn = ADAM_B2 * v_ref[...] + (1.0 - ADAM_B2) * jnp.square(gv)
        m_hat = mn / (1.0 - ADAM_B1 ** ADAM_STEP)
        v_hat = vn / (1.0 - ADAM_B2 ** ADAM_STEP)
        d_ref[...] = -ADAM_LR * (m_hat / (jnp.sqrt(v_hat) + ADAM_EPS) + ADAM_WD * w_ref[...])
        mo_ref[...] = mn
        vo_ref[...] = vn

    blk = pl.BlockSpec((tr, cols), lambda i: (i, 0))
    return pl.pallas_call(
        body, grid=(rows // tr,), in_specs=[blk] * 4, out_specs=[blk] * 3,
        out_shape=[jax.ShapeDtypeStruct((rows, cols), F32)] * 3,
        compiler_params=_cparams("parallel"), name=name,
    )(w, m, v, g)


def _adamw_nd(w, m, v, g):
    shape = w.shape
    cols = shape[-1]
    d, mn, vn = adamw(w.reshape(-1, cols), m.reshape(-1, cols), v.reshape(-1, cols), g.reshape(-1, cols))
    return d.reshape(shape), mn.reshape(shape), vn.reshape(shape)


def _pack_small(d, parts):
    flat = jnp.concatenate([p.reshape(-1).astype(F32) for p in parts])
    rows = -(-flat.shape[0] // d)
    rows = -(-rows // 8) * 8
    return jnp.pad(flat, (0, rows * d - flat.shape[0])).reshape(rows, d)


def _unpack_small(packed, shapes):
    flat = packed.reshape(-1)
    out, off = [], 0
    for s in shapes:
        n = 1
        for k in s:
            n *= k
        out.append(flat[off:off + n].reshape(s))
        off += n
    return out


def kernel(x, ffn1_norm, ffn1_w_gate, ffn1_w_up, ffn1_w_down, mix_norm, ffn2_norm, ffn2_w_gate, ffn2_w_up, ffn2_w_down, pool_w, pool_scale, kv_norm, w_k, w_v, k_gain, w_q, q_gain, rel_bias, w_o, loss_target, m_ffn1_norm, m_ffn1_w_gate, m_ffn1_w_up, m_ffn1_w_down, m_mix_norm, m_ffn2_norm, m_ffn2_w_gate, m_ffn2_w_up, m_ffn2_w_down, m_pool_w, m_pool_scale, m_kv_norm, m_w_k, m_w_v, m_k_gain, m_w_q, m_q_gain, m_rel_bias, m_w_o, v_ffn1_norm, v_ffn1_w_gate, v_ffn1_w_up, v_ffn1_w_down, v_mix_norm, v_ffn2_norm, v_ffn2_w_gate, v_ffn2_w_up, v_ffn2_w_down, v_pool_w, v_pool_scale, v_kv_norm, v_w_k, v_w_v, v_k_gain, v_w_q, v_q_gain, v_rel_bias, v_w_o):
    x0, target = x[0], loss_target[0]
    t, d = x0.shape
    depth = ffn1_norm.shape[0]
    n_b = depth - N_A_LAYERS
    fs = ffn1_w_gate.shape[-1]
    f_pad = -(-(N_DEV * fs) // TF) * TF - N_DEV * fs
    n_g = len(POOL_WINDOWS)
    cg = d // n_g
    ds_ = d // N_DEV
    me = 4 * lax.axis_index("x") + 2 * lax.axis_index("y") + lax.axis_index("c")

    ffn_in = ((ffn1_norm, ffn1_w_gate, ffn1_w_up, ffn1_w_down), (ffn2_norm, ffn2_w_gate, ffn2_w_up, ffn2_w_down))

    def ffn_gather(l, s):
        _, wg, wu, wd = ffn_in[s]
        rows = [cast_layer(wg, l, True), cast_layer(wu, l, True), cast_layer(wd, l, False)]
        return _Exchange("gather", rows, pad_rows=f_pad, two_level=True)

    scale_rows = jnp.pad(pool_scale, ((0, 8 - pool_scale.shape[0]), (0, 0)))
    mix_groups = [[pool_w[l].reshape(n_g * cg // N_DEV, cg).astype(BF16) for l in range(N_A_LAYERS)] + [scale_rows],
                  [w_k.astype(BF16), w_v.astype(BF16)]]
    mix_groups += [[w_q[b].astype(BF16), w_o[b].astype(BF16)] for b in range(n_b)]
    mix_got = {}
    bias = [band_bias(rel_bias[b]) for b in range(n_b)]

    def ffn_forward(xin, l, s):
        n = 2 * l + s
        carry = [ffn_gather((n + 1) // 2, (n + 1) % 2)] if n + 1 < 2 * depth else []
        if n < len(mix_groups):
            carry.append(_Exchange("gather", mix_groups[n], two_level=True))
        res, got = ffn_fwd(xin, ffn_in[s][0][l:l + 1], *ffn_w[s, l], carry=carry)
        if n + 1 < 2 * depth:
            ffn_w[(n + 1) % 2, (n + 1) // 2] = got[0]
        if n < len(mix_groups):
            mix_got[n] = got[-1]
        return res

    ffn_w = {(0, 0): run_exchanges([ffn_gather(0, 0)], "gather_first")[0]}
    saved = []
    xc = x0
    k_pad = v_pad = kv_saved = None
    for l in range(depth):
        rec = {"x_a": xc}
        xb, rec["hn1"], rec["g1"], rec["u1"] = ffn_forward(xc, l, 0)
        if l == 0:
            pool_f = [p.reshape(N_DEV, n_g, cg // N_DEV, cg).transpose(1, 0, 2, 3).reshape(n_g, cg, cg)
                      for p in mix_got[0][:N_A_LAYERS]]
            scale_f = mix_got[0][-1].reshape(N_DEV, 8, ds_)[:, :N_A_LAYERS].transpose(1, 0, 2).reshape(N_A_LAYERS, d)
        rec["x_b"] = xb
        if l < N_A_LAYERS:
            xm = pool_fwd(xb, mix_norm[l:l + 1], pool_f[l], scale_f[l:l + 1])
        else:
            b = l - N_A_LAYERS
            hm = rms_fwd(xb, mix_norm[l:l + 1])
            q_pre, qn = proj_nn(hm, wq_f[b], head_gain=q_gain[b:b + 1], name="proj_q")
            o = attn_fwd(qn, k_pad, v_pad, bias[b])
            xm = proj_nn(o, wo_f[b], residual=xb, name="proj_o")
            rec.update(hm=hm, q_pre=q_pre, qn=qn, o=o)
        rec["x_c"] = xm
        xc, rec["hn2"], rec["g2"], rec["u2"] = ffn_forward(xm, l, 1)
        if l == N_A_LAYERS - 1:
            wk_f, wv_f = mix_got[1]
            wq_f, wo_f = [mix_got[2 + b][0] for b in range(n_b)], [mix_got[2 + b][1] for b in range(n_b)]
            hk = rms_fwd(xc, kv_norm[None])
            k_pre, kn = proj_nn(hk, wk_f, head_gain=k_gain[None], name="proj_q")
            vv = proj_nn(hk, wv_f, out_dtype=BF16, name="proj_v")
            k_pad = jnp.pad(kn, ((LEFT, 0), (0, 0)))
            v_pad = jnp.pad(vv, ((LEFT, 0), (0, 0)))
            kv_saved = dict(x=xc, hk=hk, k_pre=k_pre)
        saved.append(rec)
    sq, dx = loss_head(xc, target)
    loss = lax.psum(0.5 * jnp.sum(sq) / d, AXES)

    g_ffn = {}
    g_norm = {0: [None] * depth, 1: [None] * depth, 2: [None] * depth}
    g_wq, g_wo, g_pool, g_qgain, g_bias, g_pscale = [None] * n_b, [None] * n_b, [None] * N_A_LAYERS, [None] * n_b, [None] * n_b, [None] * N_A_LAYERS
    dk_sum = dv_sum = None

    pending = []
    outbox = []
    mix_land = {}

    def ffn_sums(land):
        return [sum_slots(z, "sum_ffn_t" if j < 2 else "sum_ffn", transpose=j < 2) for j, z in enumerate(land)]

    def take_ffn(got):
        for (key, _), land in zip(pending, got):
            g_ffn[key] = ffn_sums(land)

    def ffn_backward(dx, l, s, x_in, hn, gate, up):
        gain = ffn_in[s][0][l:l + 1]
        ffn_carry = [_Exchange("scatter", dws, rows=fs) for _, dws in pending]
        box = list(outbox)
        outbox.clear()
        if (l, s) != (0, 0):
            (dx, dgate, dup, dy, dgain), got = ffn_bwd_x(dx, x_in, gain, gate, up, *ffn_w[s, l], carry=ffn_carry)
            take_ffn(got)
            dws, got = ffn_bwd_w(hn, dy, gate, up, dgate, dup, carry=[e for e, _ in box])
            pending[:] = [((s, l), dws)]
        else:
            wg_t, wu_t, wd_f = ffn_w[s, l]
            dgate, dup, dy = ffn_bwd_hidden(dx, gate, up, wd_f)
            dws, got = ffn_bwd_w(hn, dy, gate, up, dgate, dup, carry=ffn_carry + [e for e, _ in box])
            take_ffn(got)
            got = got[len(ffn_carry):]
            (dx, dgain), last = ffn_bwd_input(dx, x_in, gain, dgate, dup, wg_t, wu_t, carry=[_Exchange("scatter", dws, rows=fs)])
            g_ffn[s, l] = ffn_sums(last[0])
            pending.clear()
        for (_, take), res in zip(box, got):
            take(res)
        g_norm[2 * s][l] = dgain
        return dx

    def pool_grad_rows(g):
        return g.reshape(n_g, N_DEV, cg // N_DEV, cg).transpose(1, 0, 2, 3).reshape(n_g * cg, cg).astype(BF16)

    for l in reversed(range(depth)):
        rec = saved[l]
        dx = ffn_backward(dx, l, 1, rec["x_c"], rec["hn2"], rec["g2"], rec["u2"])
        if l < N_A_LAYERS:
            dx, g_pool, g_pscale[l], g_norm[1][l] = pool_bwd(rec["x_b"], dx, mix_norm[l:l + 1], pool_f[l], scale_f[l:l + 1])
            outbox.append((_Exchange("scatter", [pool_grad_rows(g_pool)], rows=ds_),
                           functools.partial(mix_land.__setitem__, ("pool", l))))
        else:
            b = l - N_A_LAYERS
            dxb = dx.astype(BF16)
            do = proj_nt(dxb, wo_f[b], out_dtype=BF16, name="proj_do")
            g_wo[b] = grad_tn(rec["o"], dxb)
            dq, dk, dv, dbias = attn_bwd(rec["qn"], k_pad, v_pad, bias[b], do)
            dk_sum = dk if dk_sum is None else dk_sum + dk
            dv_sum = dv if dv_sum is None else dv_sum + dv
            g_bias[b] = jax.vjp(band_bias, rel_bias[b])[1](dbias)[0]
            dq_pre, g_qgain[b] = head_rms_bwd(dq, rec["q_pre"], q_gain[b:b + 1])
            g_wq[b] = grad_tn(rec["hm"], dq_pre)
            dhm = proj_nt(dq_pre, wq_f[b], name="proj_dh")
            dx, g_norm[1][l] = rms_bwd(rec["x_b"], mix_norm[l:l + 1], dhm, dx)
        dx = ffn_backward(dx, l, 0, rec["x_a"], rec["hn1"], rec["g1"], rec["u1"])
        if l == N_A_LAYERS:
            dk_pre, g_kgain = head_rms_bwd(dk_sum[LEFT:], kv_saved["k_pre"], k_gain[None])
            dvb = dv_sum[LEFT:].astype(BF16)
            g_wk = grad_tn(kv_saved["hk"], dk_pre)
            g_wv = grad_tn(kv_saved["hk"], dvb)
            dhk = proj_nt(dvb, wv_f, addend=proj_nt(dk_pre, wk_f, name="proj_dh"), name="proj_dh2")
            dx, g_kvnorm = rms_bwd(kv_saved["x"], kv_norm[None], dhk, dx)
            outbox.append((_Exchange("scatter", [g_wk, g_wv] + g_wq + g_wo, rows=ds_),
                           functools.partial(mix_land.__setitem__, "att")))
    grad_x = dx[None]

    att_g = [sum_slots(z, "sum_mix") for z in mix_land["att"]]
    grad_w_k, grad_w_v = att_g[0], att_g[1]
    grad_w_q = jnp.stack(att_g[2:2 + n_b])
    grad_w_o = jnp.stack(att_g[2 + n_b:2 + 2 * n_b])
    grad_pool_w = jnp.stack([sum_slots(mix_land["pool", l][0], "sum_pool").reshape(n_g, cg // N_DEV, cg)
                             for l in range(N_A_LAYERS)])

    small_shapes = [(depth, d), (depth, d), (depth, d), (d,), (N_A_LAYERS, d), k_gain.shape, q_gain.shape, rel_bias.shape]
    small_part = _pack_small(d, [jnp.concatenate(g_norm[0]), jnp.concatenate(g_norm[1]), jnp.concatenate(g_norm[2]),
                                 g_kvnorm, jnp.concatenate(g_pscale), g_kgain, jnp.concatenate(g_qgain), jnp.stack(g_bias)])
    rows_small = small_part.shape[0]
    (small_all,), = run_exchanges([_Exchange("gather", [small_part])], "gather_small")
    small_g = sum_slots(small_all.reshape(N_DEV, rows_small, d), "sum_small")
    grad_ffn1_norm, grad_mix_norm, grad_ffn2_norm, grad_kv_norm, g_pscale_full, grad_k_gain, grad_q_gain, grad_rel_bias = \
        _unpack_small(small_g, small_shapes)
    grad_pool_scale = lax.dynamic_slice_in_dim(g_pscale_full, me * ds_, ds_, axis=1)

    def packed(ps):
        return _pack_small(d, [ps[0], ps[1], ps[2], ps[3], jnp.zeros((N_A_LAYERS, d), F32), ps[4], ps[5], ps[6]])

    small_w = (ffn1_norm, mix_norm, ffn2_norm, kv_norm, k_gain, q_gain, rel_bias)
    small_m = (m_ffn1_norm, m_mix_norm, m_ffn2_norm, m_kv_norm, m_k_gain, m_q_gain, m_rel_bias)
    small_v = (v_ffn1_norm, v_mix_norm, v_ffn2_norm, v_kv_norm, v_k_gain, v_q_gain, v_rel_bias)
    small_out = [_unpack_small(z, small_shapes) for z in adamw(packed(small_w), packed(small_m), packed(small_v), small_g, "adamw_small")]

    def ffn_grad(s, j):
        return jnp.stack([g_ffn[s, l][j] for l in range(depth)])

    grads = {
        "ffn1_w_gate": ffn_grad(0, 0), "ffn1_w_up": ffn_grad(0, 1), "ffn1_w_down": ffn_grad(0, 2),
        "ffn2_w_gate": ffn_grad(1, 0), "ffn2_w_up": ffn_grad(1, 1), "ffn2_w_down": ffn_grad(1, 2),
        "pool_w": grad_pool_w, "pool_scale": grad_pool_scale, "w_k": grad_w_k, "w_v": grad_w_v,
        "w_q": grad_w_q, "w_o": grad_w_o,
    }
    given = dict(
        ffn1_w_gate=(ffn1_w_gate, m_ffn1_w_gate, v_ffn1_w_gate), ffn1_w_up=(ffn1_w_up, m_ffn1_w_up, v_ffn1_w_up),
        ffn1_w_down=(ffn1_w_down, m_ffn1_w_down, v_ffn1_w_down), ffn2_w_gate=(ffn2_w_gate, m_ffn2_w_gate, v_ffn2_w_gate),
        ffn2_w_up=(ffn2_w_up, m_ffn2_w_up, v_ffn2_w_up), ffn2_w_down=(ffn2_w_down, m_ffn2_w_down, v_ffn2_w_down),
        pool_w=(pool_w, m_pool_w, v_pool_w), pool_scale=(pool_scale, m_pool_scale, v_pool_scale),
        w_k=(w_k, m_w_k, v_w_k), w_v=(w_v, m_w_v, v_w_v), w_q=(w_q, m_w_q, v_w_q), w_o=(w_o, m_w_o, v_w_o),
    )
    upd = {n: _adamw_nd(*given[n], grads[n]) for n in grads}
    small_names = ("ffn1_norm", "mix_norm", "ffn2_norm", "kv_norm", None, "k_gain", "q_gain", "rel_bias")
    for j, n in enumerate(small_names):
        if n is not None:
            upd[n] = tuple(small_out[z][j] for z in range(3))
    grads.update(ffn1_norm=grad_ffn1_norm, mix_norm=grad_mix_norm, ffn2_norm=grad_ffn2_norm, kv_norm=grad_kv_norm,
                 k_gain=grad_k_gain, q_gain=grad_q_gain, rel_bias=grad_rel_bias)

    order = ("ffn1_norm", "ffn1_w_gate", "ffn1_w_up", "ffn1_w_down", "mix_norm", "ffn2_norm", "ffn2_w_gate", "ffn2_w_up",
             "ffn2_w_down", "pool_w", "pool_scale", "kv_norm", "w_k", "w_v", "k_gain", "w_q", "q_gain", "rel_bias", "w_o")
    return (loss, grad_x, *[grads[n] for n in order], *[upd[n][0] for n in order], *[upd[n][1] for n in order],
            *[upd[n][2] for n in order])
```
